```python
import jax, jax.numpy as jnp
from jax import lax
import numpy as np

D_MODEL = 1024
BATCH = 8
SEQ = 2048
DEPTH = 2
DEC_BATCH = 128
DEC_SEQ = 1
PAST_LEN = 16384
PAGE_SIZE = 128

N_MIXERS = 2
N_GDN_LAYERS = (DEPTH + 1) // 2
N_POOL_LAYERS = DEPTH // 2
GDN_NK = 8
GDN_NV = 16
GDN_DK = 128
GDN_DV = 128
KEY_DIM = GDN_NK * GDN_DK
VALUE_DIM = GDN_NV * GDN_DV
CONV_W = 4
CONV_DIM = 2 * KEY_DIM + VALUE_DIM
IN_DIM = CONV_DIM + VALUE_DIM + 2 * GDN_NV
CHUNK = 64
POOL_WINDOWS = (2, 4, 8, 16)
POOL_GROUPS = len(POOL_WINDOWS)
POOL_G = D_MODEL // POOL_GROUPS
POOL_BUF = max(POOL_WINDOWS) - 1
D_FF = ((8 * D_MODEL + 3 * 256 - 1) // (3 * 256)) * 256
DN_ALPHA = (2 * DEPTH) ** 0.25
DN_BETA = (8 * DEPTH) ** -0.25
LN_EPS = 1e-5
NORM_EPS = 1e-6

kernel_name = "hybrid_gdn_pool_adaln_deepnorm_step"

F32 = jnp.float32


def layer_norm(x, g, b):
    xf = x.astype(F32)
    mu = jnp.mean(xf, -1, keepdims=True)
    var = jnp.mean(jnp.square(xf - mu), -1, keepdims=True)
    return ((xf - mu) * lax.rsqrt(var + LN_EPS) * g.astype(F32) + b.astype(F32)).astype(x.dtype)


def l2norm(x):
    return x * lax.rsqrt(jnp.sum(jnp.square(x), -1, keepdims=True) + NORM_EPS)


def causal_conv(qkv, prev, w):
    L = qkv.shape[1]
    ext = jnp.concatenate([prev.astype(qkv.dtype), qkv], axis=1)
    y = sum(ext[:, j:j + L] * w[j] for j in range(CONV_W))
    return jax.nn.silu(y), ext[:, L:]


def gated_delta_rule(q, k, v, g, beta, S0):
    B, H, L, DK = q.shape
    DV = v.shape[-1]
    C = min(CHUNK, L)
    pad = (-L) % C
    if pad:
        pw = lambda t: jnp.pad(t, [(0, 0), (0, 0), (0, pad)] + [(0, 0)] * (t.ndim - 3))
        q, k, v, g, beta = pw(q), pw(k), pw(v), pw(g), pw(beta)
    N = (L + pad) // C
    q = q.reshape(B, H, N, C, DK)
    k = k.reshape(B, H, N, C, DK)
    v = v.reshape(B, H, N, C, DV)
    beta = beta.reshape(B, H, N, C)
    g = jnp.cumsum(g.reshape(B, H, N, C), axis=-1)
    tri = jnp.tril(jnp.ones((C, C), bool))
    strict = jnp.tril(jnp.ones((C, C), bool), -1)
    decay = jnp.exp(jnp.where(tri, g[..., :, None] - g[..., None, :], -jnp.inf))
    kb = k * beta[..., None]
    A = jnp.where(strict, jnp.einsum('bhncd,bhnsd->bhncs', kb, k) * decay, 0.0)
    lhs = A + jnp.eye(C, dtype=F32)
    rhs = jnp.concatenate([v * beta[..., None], kb * jnp.exp(g)[..., None]], axis=-1)
    sol = lax.linalg.triangular_solve(lhs, rhs, left_side=True, lower=True)
    value, k_cum = sol[..., :DV], sol[..., DV:]
    attn_local = jnp.where(tri, jnp.einsum('bhncd,bhnsd->bhncs', q, k) * decay, 0.0)
    q_dec = q * jnp.exp(g)[..., None]
    k_dec = k * jnp.exp(g[..., -1:] - g)[..., None]
    g_last = jnp.exp(g[..., -1])

    def step(S, inp):
        qd, kc, val, al, kd, gl = inp
        v_new = val - jnp.einsum('bhcd,bhde->bhce', kc, S)
        o = jnp.einsum('bhcd,bhde->bhce', qd, S) + jnp.einsum('bhcs,bhse->bhce', al, v_new)
        S = S * gl[..., None, None] + jnp.einsum('bhcd,bhce->bhde', kd, v_new)
        return S, o

    mv = lambda t: jnp.moveaxis(t, 2, 0)
    S, o = lax.scan(step, S0, (mv(q_dec), mv(k_cum), mv(value), mv(attn_local), mv(k_dec), mv(g_last)))
    o = jnp.moveaxis(o, 0, 2).reshape(B, H, N * C, DV)[:, :, :L]
    return o, S


def gdn_mixer(u, S0, conv0, w_in, conv_w, A_log, dt_bias, norm_w, w_out):
    B, L, _ = u.shape
    proj = jnp.einsum('bld,de->ble', u, w_in)
    qkv = proj[..., :CONV_DIM]
    z = proj[..., CONV_DIM:CONV_DIM + VALUE_DIM]
    b = proj[..., CONV_DIM + VALUE_DIM:CONV_DIM + VALUE_DIM + GDN_NV]
    a = proj[..., CONV_DIM + VALUE_DIM + GDN_NV:]
    y, conv_new = causal_conv(qkv, conv0, conv_w)
    y = y.astype(F32)
    rep = GDN_NV // GDN_NK
    q = l2norm(y[..., :KEY_DIM].reshape(B, L, GDN_NK, GDN_DK))
    k = l2norm(y[..., KEY_DIM:2 * KEY_DIM].reshape(B, L, GDN_NK, GDN_DK))
    q = jnp.repeat(q, rep, axis=2) * (GDN_DK ** -0.5)
    k = jnp.repeat(k, rep, axis=2)
    v = y[..., 2 * KEY_DIM:].reshape(B, L, GDN_NV, GDN_DV)
    beta = jax.nn.sigmoid(b.astype(F32))
    g = -jnp.exp(A_log.astype(F32)) * jax.nn.softplus(a.astype(F32) + dt_bias.astype(F32))
    th = lambda t: jnp.swapaxes(t, 1, 2)
    o, S = gated_delta_rule(th(q), th(k), th(v), th(g), th(beta), S0.astype(F32))
    o = jnp.swapaxes(o, 1, 2)
    o = o * lax.rsqrt(jnp.mean(jnp.square(o), -1, keepdims=True) + NORM_EPS) * norm_w.astype(F32)
    o = o * jax.nn.silu(z.reshape(B, L, GDN_NV, GDN_DV).astype(F32))
    out = jnp.einsum('blv,vd->bld', o.reshape(B, L, VALUE_DIM).astype(u.dtype), w_out)
    return out, S.astype(S0.dtype), conv_new.astype(conv0.dtype)


def pool_mixer(u, prev, pos0, w_pool, pool_scale):
    B, L, D = u.shape
    uf = u.astype(F32)
    ext = jnp.concatenate([prev.astype(F32), uf], axis=1)
    cs = jnp.concatenate([jnp.zeros((B, 1, D), F32), jnp.cumsum(ext, axis=1)], axis=1)
    end = cs[:, POOL_BUF + 1:]
    pos = pos0 + jnp.arange(L)
    means = []
    for gi, w in enumerate(POOL_WINDOWS):
        sl = slice(gi * POOL_G, (gi + 1) * POOL_G)
        start = cs[:, POOL_BUF + 1 - w:POOL_BUF + 1 - w + L, sl]
        cnt = jnp.minimum(pos + 1, w).astype(F32)[None, :, None]
        means.append((end[..., sl] - start) / cnt)
    pooled = jnp.concatenate(means, axis=-1) - uf
    h = jnp.einsum('blgc,gce->blge', pooled.reshape(B, L, POOL_GROUPS, POOL_G), w_pool.astype(F32))
    h = h.reshape(B, L, D) * pool_scale.astype(F32)
    return h.astype(u.dtype), ext[:, L:].astype(prev.dtype)


def swiglu(u, w_up, w_down):
    h = jnp.einsum('bld,df->blf', u, w_up)
    gt, up = h[..., :D_FF], h[..., D_FF:]
    return jnp.einsum('blf,fd->bld', jax.nn.silu(gt) * up, w_down)


def run_group(x, c, gdn_S, gdn_conv, pool_buf, pos0, ada_w, ada_b, ln_g, ln_b,
              gdn_w_in, gdn_conv_w, gdn_A_log, gdn_dt_bias, gdn_norm_w, gdn_w_out,
              pool_w, pool_scale, ffn_w_up, ffn_w_down):
    mod = jnp.einsum('bd,lde->lbe', jax.nn.silu(c), ada_w) + ada_b[:, None, :]
    y = x
    new_S, new_conv, new_pool = [], [], []
    for i in range(DEPTH):
        sh_m, sc_m, ga_m, sh_f, sc_f, ga_f = jnp.split(mod[i][:, None, :], 6, axis=-1)
        u = y * (1 + sc_m) + sh_m
        j = i // N_MIXERS
        if i % N_MIXERS == 0:
            h, S, cv = gdn_mixer(u, gdn_S[j], gdn_conv[j], gdn_w_in[j], gdn_conv_w[j], gdn_A_log[j],
                                 gdn_dt_bias[j], gdn_norm_w[j], gdn_w_out[j])
            new_S.append(S)
            new_conv.append(cv)
        else:
            h, pb = pool_mixer(u, pool_buf[j], pos0, pool_w[j], pool_scale[j])
            new_pool.append(pb)
        y = layer_norm(DN_ALPHA * y + ga_m * h, ln_g[i, 0], ln_b[i, 0])
        u = y * (1 + sc_f) + sh_f
        y = layer_norm(DN_ALPHA * y + ga_f * swiglu(u, ffn_w_up[i], ffn_w_down[i]), ln_g[i, 1], ln_b[i, 1])
    return y, jnp.stack(new_S), jnp.stack(new_conv), jnp.stack(new_pool)


def setup_inputs(seed: int = 0) -> dict:
    key = jax.random.key(seed)
    ks = jax.random.split(key, 24)
    nrm = lambda k, s, sc: jax.random.normal(k, s, F32) * sc
    D = D_MODEL
    ada_b = jnp.concatenate([
        nrm(ks[5], (DEPTH, 2 * D), 0.02),
        1.0 + nrm(ks[6], (DEPTH, D), 0.02),
        nrm(ks[7], (DEPTH, 2 * D), 0.02),
        1.0 + nrm(ks[8], (DEPTH, D), 0.02)], axis=-1)
    w_in = nrm(ks[9], (N_GDN_LAYERS, D, IN_DIM), D ** -0.5)
    v_cols = (jnp.arange(IN_DIM) >= 2 * KEY_DIM) & (jnp.arange(IN_DIM) < CONV_DIM)
    w_in = w_in * jnp.where(v_cols, DN_BETA, 1.0)
    return {
        "x_prompt": nrm(ks[0], (BATCH, SEQ, D), 1.0),
        "x_sample": nrm(ks[1], (DEC_BATCH, DEC_SEQ, D), 1.0),
        "c_prompt": nrm(ks[2], (BATCH, D), 1.0),
        "c_sample": nrm(ks[3], (DEC_BATCH, D), 1.0),
        "state_gdn_S": nrm(ks[4], (N_GDN_LAYERS, DEC_BATCH, GDN_NV, GDN_DK, GDN_DV), 0.1),
        "state_gdn_conv": nrm(ks[10], (N_GDN_LAYERS, DEC_BATCH, CONV_W - 1, CONV_DIM), 1.0),
        "state_pool": nrm(ks[11], (N_POOL_LAYERS, DEC_BATCH, POOL_BUF, D), 1.0),
        "ada_w": nrm(ks[12], (DEPTH, D, 6 * D), 0.1 * D ** -0.5),
        "ada_b": ada_b,
        "ln_g": 1.0 + nrm(ks[13], (DEPTH, 2, D), 0.02),
        "ln_b": nrm(ks[14], (DEPTH, 2, D), 0.02),
        "gdn_w_in": w_in,
        "gdn_conv_w": nrm(ks[15], (N_GDN_LAYERS, CONV_W, CONV_DIM), CONV_W ** -0.5),
        "gdn_A_log": jnp.log(jax.random.uniform(ks[16], (N_GDN_LAYERS, GDN_NV), F32, 1.0, 16.0)),
        "gdn_dt_bias": nrm(ks[17], (N_GDN_LAYERS, GDN_NV), 0.1),
        "gdn_norm_w": 1.0 + nrm(ks[18], (N_GDN_LAYERS, GDN_DV), 0.02),
        "gdn_w_out": nrm(ks[19], (N_GDN_LAYERS, VALUE_DIM, D), DN_BETA * VALUE_DIM ** -0.5),
        "pool_w": nrm(ks[20], (N_POOL_LAYERS, POOL_GROUPS, POOL_G, POOL_G), DN_BETA * POOL_G ** -0.5),
        "pool_scale": 1.0 + nrm(ks[21], (N_POOL_LAYERS, D), 0.02),
        "ffn_w_up": nrm(ks[22], (DEPTH, D, 2 * D_FF), D ** -0.5),
        "ffn_w_down": nrm(ks[23], (DEPTH, D_FF, D), DN_BETA * D_FF ** -0.5),
    }


def reference(x_prompt, x_sample, c_prompt, c_sample, state_gdn_S, state_gdn_conv, state_pool,
              ada_w, ada_b, ln_g, ln_b, gdn_w_in, gdn_conv_w, gdn_A_log, gdn_dt_bias, gdn_norm_w,
              gdn_w_out, pool_w, pool_scale, ffn_w_up, ffn_w_down):
    B = x_prompt.shape[0]
    dt = x_prompt.dtype
    zS = jnp.zeros((N_GDN_LAYERS, B, GDN_NV, GDN_DK, GDN_DV), dt)
    zC = jnp.zeros((N_GDN_LAYERS, B, CONV_W - 1, CONV_DIM), dt)
    zP = jnp.zeros((N_POOL_LAYERS, B, POOL_BUF, D_MODEL), dt)
    y_prompt, p_S, p_conv, p_pool = run_group(
        x_prompt, c_prompt, zS, zC, zP, 0, ada_w, ada_b, ln_g, ln_b, gdn_w_in, gdn_conv_w,
        gdn_A_log, gdn_dt_bias, gdn_norm_w, gdn_w_out, pool_w, pool_scale, ffn_w_up, ffn_w_down)
    y_sample, s_S, s_conv, s_pool = run_group(
        x_sample, c_sample, state_gdn_S, state_gdn_conv, state_pool, PAST_LEN, ada_w, ada_b, ln_g, ln_b,
        gdn_w_in, gdn_conv_w, gdn_A_log, gdn_dt_bias, gdn_norm_w, gdn_w_out, pool_w, pool_scale,
        ffn_w_up, ffn_w_down)
    return (y_prompt, y_sample, p_S, p_conv, p_pool, s_S, s_conv, s_pool)
```

```python
import functools

import jax
import jax.numpy as jnp
from jax import lax
from jax.experimental import pallas as pl
from jax.experimental.pallas import tpu as pltpu

F32 = jnp.float32
BF16 = jnp.bfloat16

D_MODEL = 1024
DEPTH = 2
GDN_NK = 8
GDN_NV = 16
GDN_DK = 128
GDN_DV = 128
KEY_DIM = GDN_NK * GDN_DK
VALUE_DIM = GDN_NV * GDN_DV
CONV_W = 4
CONV_DIM = 2 * KEY_DIM + VALUE_DIM
QKVZ_DIM = CONV_DIM + VALUE_DIM
CHUNK = 64
POOL_WINDOWS = (2, 4, 8, 16)
POOL_G = D_MODEL // len(POOL_WINDOWS)
POOL_BUF = max(POOL_WINDOWS) - 1
D_FF = 2816
DN_ALPHA = (2 * DEPTH) ** 0.25
LN_EPS = 1e-5
NORM_EPS = 1e-6
PAST_LEN = 16384

LANES = 128
SUBLANES = 8
N_COLBLK = QKVZ_DIM // LANES
N_CONVBLK = CONV_DIM // LANES
GATE_W = 2 * LANES
FF_SPLIT = 2
VMEM_LIMIT = 56 * 1024 * 1024


def _bdot(a, b):
    return jnp.dot(a.astype(BF16), b.astype(BF16), preferred_element_type=F32)


def _silu(x):
    return x * jax.nn.sigmoid(x)


def _softplus(x):
    return jnp.maximum(x, 0.0) + jnp.log1p(jnp.exp(-jnp.abs(x)))


def _layer_norm(x, g, b):
    mu = jnp.mean(x, axis=-1, keepdims=True)
    xc = x - mu
    var = jnp.mean(xc * xc, axis=-1, keepdims=True)
    return xc * lax.rsqrt(var + LN_EPS) * g + b


def _const_spec(shape):
    nd = len(shape)
    return pl.BlockSpec(shape, lambda *_: (0,) * nd, pipeline_mode=pl.Buffered(1))


def _ada_kernel(c_ref, w_ref, b_ref, o_ref):
    c = c_ref[...]
    o_ref[0] = _bdot(_silu(c), w_ref[0]) + b_ref[0]


def _ada(c_all, ada_w, ada_b):
    n = c_all.shape[0]
    tn = 1536
    return pl.pallas_call(
        _ada_kernel,
        grid=(DEPTH, 6 * D_MODEL // tn),
        in_specs=[
            pl.BlockSpec((n, D_MODEL), lambda l, j: (0, 0)),
            pl.BlockSpec((1, D_MODEL, tn), lambda l, j: (l, 0, j)),
            pl.BlockSpec((1, 1, tn), lambda l, j: (l, 0, j)),
        ],
        out_specs=pl.BlockSpec((1, n, tn), lambda l, j: (l, 0, j)),
        out_shape=jax.ShapeDtypeStruct((DEPTH, n, 6 * D_MODEL), F32),
        compiler_params=pltpu.CompilerParams(
            dimension_semantics=("arbitrary", "arbitrary"), vmem_limit_bytes=VMEM_LIMIT),
        name="ada_mod",
    )(c_all, ada_w, ada_b.reshape(DEPTH, 1, 6 * D_MODEL))


def _proj_kernel(x_ref, mod_ref, w_ref, wg_ref, o_ref, g_ref):
    x = x_ref[0]
    sh = mod_ref[0, :, 0:D_MODEL]
    sc = mod_ref[0, :, D_MODEL:2 * D_MODEL]
    u = (x * (1.0 + sc) + sh).astype(BF16)
    step = 4
    for j in range(0, N_COLBLK, step):
        res = jnp.dot(u, w_ref[:, j * LANES:(j + step) * LANES], preferred_element_type=F32)
        for i in range(step):
            o_ref[0, j + i] = res[:, i * LANES:(i + 1) * LANES]
    g_ref[0] = jnp.dot(u, wg_ref[...], preferred_element_type=F32)


def _proj(x, mod, w_qkvz, w_gate, tm):
    b, l, _ = x.shape
    r = mod.shape[1]
    return pl.pallas_call(
        _proj_kernel,
        grid=(b, l // tm),
        in_specs=[
            pl.BlockSpec((1, tm, D_MODEL), lambda i, j: (i, j, 0)),
            pl.BlockSpec((1, r, 6 * D_MODEL), lambda i, j: (i, 0, 0)),
            _const_spec((D_MODEL, QKVZ_DIM)),
            _const_spec((D_MODEL, GATE_W)),
        ],
        out_specs=[
            pl.BlockSpec((1, N_COLBLK, tm, LANES), lambda i, j: (i, 0, j, 0)),
            pl.BlockSpec((1, tm, GATE_W), lambda i, j: (i, j, 0)),
        ],
        out_shape=[
            jax.ShapeDtypeStruct((b, N_COLBLK, l, LANES), F32),
            jax.ShapeDtypeStruct((b, l, GATE_W), F32),
        ],
        compiler_params=pltpu.CompilerParams(
            dimension_semantics=("arbitrary", "arbitrary"), vmem_limit_bytes=VMEM_LIMIT),
        name="gdn_proj",
    )(x, mod, w_qkvz, w_gate)


def _gdn_chunk_kernel(p_ref, g_ref, cw_ref, alog_ref, dtb_ref, nw_ref, o_ref, s_out_ref, s_scr, ext_scr):
    c_idx = pl.program_id(1)
    C = CHUNK
    HALO = SUBLANES

    @pl.when(c_idx == 0)
    def _():
        s_scr[...] = jnp.zeros_like(s_scr)
        ext_scr[:, 0:HALO, :] = jnp.zeros((N_CONVBLK, HALO, LANES), F32)

    cw = cw_ref[...]
    conv = []
    for j in range(N_CONVBLK):
        ext_scr[j, HALO:HALO + C, :] = p_ref[0, j]
        acc = None
        for t in range(CONV_W):
            term = ext_scr[j, HALO - (CONV_W - 1) + t:HALO - (CONV_W - 1) + t + C, :] * \
                cw[t:t + 1, j * LANES:(j + 1) * LANES]
            acc = term if acc is None else acc + term
        conv.append(_silu(acc))
        ext_scr[j, 0:HALO, :] = ext_scr[j, C:C + HALO, :]

    def l2n(y):
        return y * lax.rsqrt(jnp.sum(y * y, axis=-1, keepdims=True) + NORM_EPS)

    q = [l2n(conv[j]) * (GDN_DK ** -0.5) for j in range(GDN_NK)]
    k = [l2n(conv[GDN_NK + j]) for j in range(GDN_NK)]
    v = conv[2 * GDN_NK:]

    beta = jax.nn.sigmoid(g_ref[0, :, 0:LANES])
    g = -jnp.exp(alog_ref[...]) * _softplus(g_ref[0, :, LANES:2 * LANES] + dtb_ref[...])
    r2 = lax.broadcasted_iota(jnp.int32, (C, 2 * C), 0)
    c2 = lax.broadcasted_iota(jnp.int32, (C, 2 * C), 1)
    cm = jnp.where(c2 >= C, c2 - C, c2)
    left = c2 < C
    incl = r2 >= cm
    strict = r2 > cm
    ident_left = jnp.where(left & (r2 == cm), 1.0, 0.0).astype(F32)
    zeros_c = jnp.zeros((C, LANES), F32)
    ltri = jnp.where(left & incl, 1.0, 0.0).astype(F32)
    gc = jnp.dot(ltri, jnp.concatenate([g, zeros_c], axis=0), preferred_element_type=F32,
                 precision=lax.Precision.HIGHEST)
    eye = (lax.broadcasted_iota(jnp.int32, (LANES, LANES), 0) ==
           lax.broadcasted_iota(jnp.int32, (LANES, LANES), 1)).astype(F32)
    gc_rows = lax.dot_general(eye, jnp.concatenate([gc, gc], axis=0), (((1,), (1,)), ((), ())),
                              preferred_element_type=F32, precision=lax.Precision.HIGHEST)
    egc = jnp.exp(gc)
    g_last = gc[C - 1:C, :]
    edl = jnp.exp(g_last - gc)
    egl = jnp.exp(g_last)

    for p in range(GDN_NK):
        kp, qp = k[p], q[p]
        kq = jnp.concatenate([kp, qp], axis=0).astype(BF16)
        kk = jnp.concatenate([kp, kp], axis=0).astype(BF16)
        gram = lax.dot_general(kq, kk, (((1,), (1,)), ((), ())), preferred_element_type=F32)
        s_pair = jnp.concatenate([s_scr[2 * p], s_scr[2 * p + 1]], axis=1)
        kqs = _bdot(kq, s_pair)
        kt_pad = jnp.concatenate([kp, zeros_c], axis=0).T
        vv = []
        for hh in range(2):
            h = 2 * p + hh
            beta_c = beta[:, h:h + 1]
            egc_c = egc[:, h:h + 1]
            edl_c = edl[:, h:h + 1]
            diff = gc[:, h:h + 1] - gc_rows[h:h + 1, :]
            decay = jnp.exp(jnp.where(incl, diff, -jnp.inf))
            neg_a = jnp.where(strict, -(beta_c * gram[0:C] * decay), 0.0)
            attn = jnp.where(left, gram[C:2 * C] * decay, 0.0)
            lhs = neg_a
            rhs_b = jnp.where(left, ident_left, neg_a)
            for _ in range(6):
                out = _bdot(lhs, jnp.concatenate([zeros_c, rhs_b], axis=0))
                rhs_b = jnp.where(left, rhs_b + out, out)
                lhs = out
            t_inv = jnp.where(left, rhs_b, 0.0)
            ks = kqs[0:C, hh * LANES:(hh + 1) * LANES]
            qs = kqs[C:2 * C, hh * LANES:(hh + 1) * LANES]
            w = beta_c * (v[h] - egc_c * ks)
            v_new = _bdot(t_inv, jnp.concatenate([w, zeros_c], axis=0))
            o = egc_c * qs + _bdot(attn, jnp.concatenate([v_new, zeros_c], axis=0))
            vv.append(edl_c * v_new)
            o = o * lax.rsqrt(jnp.mean(o * o, axis=-1, keepdims=True) + NORM_EPS) * nw_ref[...]
            o = o * _silu(p_ref[0, N_CONVBLK + h])
            o_ref[0, :, h * LANES:(h + 1) * LANES] = o.astype(BF16)
        vv2 = jnp.concatenate([jnp.concatenate(vv, axis=1), jnp.zeros((C, 2 * LANES), F32)], axis=0)
        ds = _bdot(kt_pad, vv2)
        for hh in range(2):
            h = 2 * p + hh
            s_scr[h] = s_scr[h] * egl[:, h:h + 1] + ds[:, hh * LANES:(hh + 1) * LANES]

    @pl.when(c_idx == pl.num_programs(1) - 1)
    def _():
        s_out_ref[0] = s_scr[...]


def _gdn_prompt(proj, gates, conv_w, a_log, dt_bias, norm_w):
    b, _, l, _ = proj.shape
    nc = l // CHUNK
    return pl.pallas_call(
        _gdn_chunk_kernel,
        grid=(b, nc),
        in_specs=[
            pl.BlockSpec((1, N_COLBLK, CHUNK, LANES), lambda i, c: (i, 0, c, 0)),
            pl.BlockSpec((1, CHUNK, GATE_W), lambda i, c: (i, c, 0)),
            _const_spec((CONV_W, CONV_DIM)),
            _const_spec((1, LANES)),
            _const_spec((1, LANES)),
            _const_spec((1, LANES)),
        ],
        out_specs=[
            pl.BlockSpec((1, CHUNK, VALUE_DIM), lambda i, c: (i, c, 0)),
            pl.BlockSpec((1, GDN_NV, GDN_DK, GDN_DV), lambda i, c: (i, 0, 0, 0)),
        ],
        out_shape=[
            jax.ShapeDtypeStruct((b, l, VALUE_DIM), BF16),
            jax.ShapeDtypeStruct((b, GDN_NV, GDN_DK, GDN_DV), F32),
        ],
        scratch_shapes=[
            pltpu.VMEM((GDN_NV, GDN_DK, GDN_DV), F32),
            pltpu.VMEM((N_CONVBLK, CHUNK + SUBLANES, LANES), F32),
        ],
        compiler_params=pltpu.CompilerParams(
            dimension_semantics=("arbitrary", "arbitrary"), vmem_limit_bytes=VMEM_LIMIT),
        name="gdn_chunk",
    )(proj, gates, conv_w, a_log, dt_bias, norm_w)


def _mixer_norm_ffn(y_in, h_mix, mod, ln, wu_ref, wd_ref):
    ga = mod[:, 2 * D_MODEL:3 * D_MODEL]
    sh_f = mod[:, 3 * D_MODEL:4 * D_MODEL]
    sc_f = mod[:, 4 * D_MODEL:5 * D_MODEL]
    ga_f = mod[:, 5 * D_MODEL:6 * D_MODEL]
    y1 = _layer_norm(DN_ALPHA * y_in + ga * h_mix, ln[0:1], ln[1:2])
    u = (y1 * (1.0 + sc_f) + sh_f).astype(BF16)
    fw = D_FF // FF_SPLIT
    ff = None
    for s in range(FF_SPLIT):
        gt = jnp.dot(u, wu_ref[:, s * fw:(s + 1) * fw], preferred_element_type=F32)
        up = jnp.dot(u, wu_ref[:, D_FF + s * fw:D_FF + (s + 1) * fw], preferred_element_type=F32)
        part = jnp.dot((_silu(gt) * up).astype(BF16), wd_ref[s * fw:(s + 1) * fw, :],
                       preferred_element_type=F32)
        ff = part if ff is None else ff + part
    return _layer_norm(DN_ALPHA * y1 + ga_f * ff, ln[2:3], ln[3:4])


def _post_gdn_kernel(x_ref, og_ref, mod_ref, ln_ref, wo_ref, wu_ref, wd_ref, y_ref):
    h_mix = _bdot(og_ref[0], wo_ref[...])
    y_ref[0] = _mixer_norm_ffn(x_ref[0], h_mix, mod_ref[0], ln_ref[...], wu_ref, wd_ref)


def _post_gdn(x, og, mod, ln, w_out, w_up, w_down, tm):
    b, l, _ = x.shape
    r = mod.shape[1]
    return pl.pallas_call(
        _post_gdn_kernel,
        grid=(b, l // tm),
        in_specs=[
            pl.BlockSpec((1, tm, D_MODEL), lambda i, j: (i, j, 0)),
            pl.BlockSpec((1, tm, VALUE_DIM), lambda i, j: (i, j, 0)),
            pl.BlockSpec((1, r, 6 * D_MODEL), lambda i, j: (i, 0, 0)),
            _const_spec((4, D_MODEL)),
            _const_spec((VALUE_DIM, D_MODEL)),
            _const_spec((D_MODEL, 2 * D_FF)),
            _const_spec((D_FF, D_MODEL)),
        ],
        out_specs=pl.BlockSpec((1, tm, D_MODEL), lambda i, j: (i, j, 0)),
        out_shape=jax.ShapeDtypeStruct((b, l, D_MODEL), F32),
        compiler_params=pltpu.CompilerParams(
            dimension_semantics=("arbitrary", "arbitrary"), vmem_limit_bytes=VMEM_LIMIT),
        name="post_gdn_ffn",
    )(x, og, mod, ln, w_out, w_up, w_down)


def _pool_project(pooled, pw_ref, ps):
    parts = [_bdot(pooled[:, gi * POOL_G:(gi + 1) * POOL_G], pw_ref[gi]) for gi in range(len(POOL_WINDOWS))]
    return jnp.concatenate(parts, axis=1) * ps


def _pool_layer_kernel(y_ref, mod_ref, ln_ref, pw_ref, ps_ref, wu_ref, wd_ref, o_ref, pool_ref, ext_scr, *, tm):
    j = pl.program_id(1)
    HALO = 2 * SUBLANES

    @pl.when(j == 0)
    def _():
        ext_scr[0:HALO, :] = jnp.zeros((HALO, D_MODEL), F32)

    y = y_ref[0]
    mod = mod_ref[0]
    u = y * (1.0 + mod[:, D_MODEL:2 * D_MODEL]) + mod[:, 0:D_MODEL]
    ext_scr[HALO:HALO + tm, :] = u
    pos = j * tm + lax.broadcasted_iota(jnp.int32, (tm, 1), 0)
    means = []
    for gi, w in enumerate(POOL_WINDOWS):
        cols = slice(gi * POOL_G, (gi + 1) * POOL_G)
        acc = u[:, cols]
        for m in range(1, w):
            acc = acc + ext_scr[HALO - m:HALO - m + tm, cols]
        cnt = jnp.minimum(pos + 1, w).astype(F32)
        means.append(acc / cnt)
    pooled = jnp.concatenate(means, axis=1) - u
    h_mix = _pool_project(pooled, pw_ref, ps_ref[...])
    o_ref[0] = _mixer_norm_ffn(y, h_mix, mod, ln_ref[...], wu_ref, wd_ref)
    tail = ext_scr[tm:tm + HALO, :]
    ext_scr[0:HALO, :] = tail

    @pl.when(j == pl.num_programs(1) - 1)
    def _():
        pool_ref[0] = tail


def _pool_layer(y, mod, ln, pool_w, pool_scale, w_up, w_down, tm):
    b, l, _ = y.shape
    return pl.pallas_call(
        functools.partial(_pool_layer_kernel, tm=tm),
        grid=(b, l // tm),
        in_specs=[
            pl.BlockSpec((1, tm, D_MODEL), lambda i, j: (i, j, 0)),
            pl.BlockSpec((1, 1, 6 * D_MODEL), lambda i, j: (i, 0, 0)),
            _const_spec((4, D_MODEL)),
            _const_spec((len(POOL_WINDOWS), POOL_G, POOL_G)),
            _const_spec((1, D_MODEL)),
            _const_spec((D_MODEL, 2 * D_FF)),
            _const_spec((D_FF, D_MODEL)),
        ],
        out_specs=[
            pl.BlockSpec((1, tm, D_MODEL), lambda i, j: (i, j, 0)),
            pl.BlockSpec((1, 2 * SUBLANES, D_MODEL), lambda i, j: (i, 0, 0)),
        ],
        out_shape=[
            jax.ShapeDtypeStruct((b, l, D_MODEL), F32),
            jax.ShapeDtypeStruct((b, 2 * SUBLANES, D_MODEL), F32),
        ],
        scratch_shapes=[pltpu.VMEM((tm + 2 * SUBLANES, D_MODEL), F32)],
        compiler_params=pltpu.CompilerParams(
            dimension_semantics=("arbitrary", "arbitrary"), vmem_limit_bytes=VMEM_LIMIT),
        name="pool_ffn",
    )(y, mod, ln, pool_w, pool_scale, w_up, w_down)


def _gdn_step_pre_kernel(p_ref, g_ref, cs_ref, cw_ref, alog_ref, dtb_ref, q_ref, k_ref, v_ref, z_ref, cn_ref,
                         beta_ref, eg_ref):
    cw = cw_ref[...]
    for h in range(GDN_NV):
        z_ref[:, h * LANES:(h + 1) * LANES] = p_ref[0, N_CONVBLK + h]
    for j in range(N_CONVBLK):
        cols = slice(j * LANES, (j + 1) * LANES)
        cur = p_ref[0, j]
        acc = cur * cw[CONV_W - 1:CONV_W, cols]
        for t in range(CONV_W - 1):
            acc = acc + cs_ref[:, t * CONV_DIM + j * LANES:t * CONV_DIM + (j + 1) * LANES] * cw[t:t + 1, cols]
        y = _silu(acc)
        if j < 2 * GDN_NK:
            y = y * lax.rsqrt(jnp.sum(y * y, axis=-1, keepdims=True) + NORM_EPS)
        if j < GDN_NK:
            q_ref[:, cols] = y * (GDN_DK ** -0.5)
        elif j < 2 * GDN_NK:
            k_ref[:, (j - GDN_NK) * LANES:(j - GDN_NK + 1) * LANES] = y
        else:
            v_ref[:, (j - 2 * GDN_NK) * LANES:(j - 2 * GDN_NK + 1) * LANES] = y
        cn_ref[:, (CONV_W - 2) * CONV_DIM + j * LANES:(CONV_W - 2) * CONV_DIM + (j + 1) * LANES] = cur
    cn_ref[:, 0:(CONV_W - 2) * CONV_DIM] = cs_ref[:, CONV_DIM:(CONV_W - 1) * CONV_DIM]
    beta_ref[...] = jax.nn.sigmoid(g_ref[0, :, 0:LANES])
    g = -jnp.exp(alog_ref[...]) * _softplus(g_ref[0, :, LANES:2 * LANES] + dtb_ref[...])
    eg_ref[...] = jnp.exp(g)


def _gdn_step_pre(proj, gates, conv_state, conv_w, a_log, dt_bias):
    n = proj.shape[2]
    vm = pl.BlockSpec(memory_space=pltpu.VMEM)
    return pl.pallas_call(
        _gdn_step_pre_kernel,
        in_specs=[vm] * 6,
        out_specs=[vm] * 7,
        out_shape=[
            jax.ShapeDtypeStruct((n, KEY_DIM), F32),
            jax.ShapeDtypeStruct((n, KEY_DIM), F32),
            jax.ShapeDtypeStruct((n, VALUE_DIM), F32),
            jax.ShapeDtypeStruct((n, VALUE_DIM), F32),
            jax.ShapeDtypeStruct((n, (CONV_W - 1) * CONV_DIM), F32),
            jax.ShapeDtypeStruct((n, LANES), F32),
            jax.ShapeDtypeStruct((n, LANES), F32),
        ],
        compiler_params=pltpu.CompilerParams(vmem_limit_bytes=VMEM_LIMIT),
        name="gdn_step_pre",
    )(proj, gates, conv_state, conv_w, a_log, dt_bias)


def _gdn_step_kernel(beta_ref, eg_ref, q_ref, k_ref, v_ref, z_ref, nw_ref, s_ref, o_ref, sn_ref, *, bt):
    i = pl.program_id(0)
    pad = jnp.zeros((LANES - bt, LANES), F32)
    nw = nw_ref[...]
    for p in range(GDN_NK):
        cols = slice(p * LANES, (p + 1) * LANES)
        kp = k_ref[:, cols]
        qp = q_ref[:, cols]
        kt = jnp.concatenate([kp, pad], axis=0).T
        qt = jnp.concatenate([qp, pad], axis=0).T
        qk = jnp.sum(qp * kp, axis=-1, keepdims=True)
        for b in range(bt):
            kcol = kt[:, b:b + 1]
            qcol = qt[:, b:b + 1]
            for hh in range(2):
                h = 2 * p + hh
                beta = beta_ref[(i * bt + b) * GDN_NV + h]
                eg = eg_ref[(i * bt + b) * GDN_NV + h]
                hc = slice(h * LANES, (h + 1) * LANES)
                s = s_ref[b, h]
                ks = jnp.sum(s * kcol, axis=0, keepdims=True)
                qs = jnp.sum(s * qcol, axis=0, keepdims=True)
                v_new = beta * (v_ref[b:b + 1, hc] - eg * ks)
                o = eg * qs + qk[b:b + 1, :] * v_new
                sn_ref[b, h] = s * eg + kcol * v_new
                o = o * lax.rsqrt(jnp.mean(o * o, axis=-1, keepdims=True) + NORM_EPS) * nw
                o = o * _silu(z_ref[b:b + 1, hc])
                o_ref[b:b + 1, hc] = o


def _gdn_step(beta, eg, q, k, v, z, norm_w, s_state, bt):
    n = q.shape[0]
    smem = pl.BlockSpec(memory_space=pltpu.SMEM)
    return pl.pallas_call(
        functools.partial(_gdn_step_kernel, bt=bt),
        grid=(n // bt,),
        in_specs=[
            smem, smem,
            pl.BlockSpec((bt, KEY_DIM), lambda i: (i, 0)),
            pl.BlockSpec((bt, KEY_DIM), lambda i: (i, 0)),
            pl.BlockSpec((bt, VALUE_DIM), lambda i: (i, 0)),
            pl.BlockSpec((bt, VALUE_DIM), lambda i: (i, 0)),
            _const_spec((1, LANES)),
            pl.BlockSpec((bt, GDN_NV, GDN_DK, GDN_DV), lambda i: (i, 0, 0, 0)),
        ],
        out_specs=[
            pl.BlockSpec((bt, VALUE_DIM), lambda i: (i, 0)),
            pl.BlockSpec((bt, GDN_NV, GDN_DK, GDN_DV), lambda i: (i, 0, 0, 0)),
        ],
        out_shape=[
            jax.ShapeDtypeStruct((n, VALUE_DIM), F32),
            jax.ShapeDtypeStruct(s_state.shape, F32),
        ],
        compiler_params=pltpu.CompilerParams(
            dimension_semantics=("arbitrary",), vmem_limit_bytes=VMEM_LIMIT),
        name="gdn_step",
    )(beta, eg, q, k, v, z, norm_w, s_state)


def _pool_step_kernel(y_ref, mod_ref, ln_ref, ps_state_ref, pw_ref, ps_ref, wu_ref, wd_ref, o_ref, pn_ref,
                      *, pos0):
    y = y_ref[0]
    mod = mod_ref[0]
    u = y * (1.0 + mod[:, D_MODEL:2 * D_MODEL]) + mod[:, 0:D_MODEL]
    means = []
    for gi, w in enumerate(POOL_WINDOWS):
        acc = u[:, gi * POOL_G:(gi + 1) * POOL_G]
        for m in range(1, w):
            base = (POOL_BUF - m) * D_MODEL + gi * POOL_G
            acc = acc + ps_state_ref[:, base:base + POOL_G]
        means.append(acc / float(min(pos0 + 1, w)))
    pooled = jnp.concatenate(means, axis=1) - u
    h_mix = _pool_project(pooled, pw_ref, ps_ref[...])
    o_ref[0] = _mixer_norm_ffn(y, h_mix, mod, ln_ref[...], wu_ref, wd_ref)
    pn_ref[:, 0:(POOL_BUF - 1) * D_MODEL] = ps_state_ref[:, D_MODEL:POOL_BUF * D_MODEL]
    pn_ref[:, (POOL_BUF - 1) * D_MODEL:POOL_BUF * D_MODEL] = u


def _pool_step(y, mod, ln, pool_state, pool_w, pool_scale, w_up, w_down, pos0):
    n = y.shape[1]
    vm = pl.BlockSpec(memory_space=pltpu.VMEM)
    return pl.pallas_call(
        functools.partial(_pool_step_kernel, pos0=pos0),
        in_specs=[vm] * 8,
        out_specs=[vm, vm],
        out_shape=[
            jax.ShapeDtypeStruct((1, n, D_MODEL), F32),
            jax.ShapeDtypeStruct((n, POOL_BUF * D_MODEL), F32),
        ],
        compiler_params=pltpu.CompilerParams(vmem_limit_bytes=VMEM_LIMIT),
        name="pool_step_ffn",
    )(y, mod, ln, pool_state, pool_w, pool_scale, w_up, w_down)


def kernel(x_prompt, x_sample, c_prompt, c_sample, state_gdn_S, state_gdn_conv, state_pool, ada_w, ada_b,
           ln_g, ln_b, gdn_w_in, gdn_conv_w, gdn_A_log, gdn_dt_bias, gdn_norm_w, gdn_w_out, pool_w,
           pool_scale, ffn_w_up, ffn_w_down):
    nb, seq, _ = x_prompt.shape
    ns = x_sample.shape[0]

    w_in = gdn_w_in[0]
    w_qkvz = w_in[:, :QKVZ_DIM].astype(BF16)
    w_b = w_in[:, QKVZ_DIM:QKVZ_DIM + GDN_NV]
    w_a = w_in[:, QKVZ_DIM + GDN_NV:]
    lane_pad = ((0, 0), (0, LANES - GDN_NV))
    w_gate = jnp.concatenate([jnp.pad(w_b, lane_pad), jnp.pad(w_a, lane_pad)], axis=1).astype(BF16)
    a_log = jnp.pad(gdn_A_log, lane_pad)
    dt_bias = jnp.pad(gdn_dt_bias, lane_pad)
    w_out = gdn_w_out[0].astype(BF16)
    w_up = ffn_w_up.astype(BF16)
    w_down = ffn_w_down.astype(BF16)
    pw = pool_w[0].astype(BF16)
    ln = jnp.stack([ln_g[:, 0], ln_b[:, 0], ln_g[:, 1], ln_b[:, 1]], axis=1)

    mod = _ada(jnp.concatenate([c_prompt, c_sample], axis=0), ada_w, ada_b)
    mod_p = [mod[i, :nb].reshape(nb, 1, 6 * D_MODEL) for i in range(DEPTH)]
    mod_s = [mod[i, nb:].reshape(1, ns, 6 * D_MODEL) for i in range(DEPTH)]

    proj, gates = _proj(x_prompt, mod_p[0], w_qkvz, w_gate, tm=256)
    og, p_S = _gdn_prompt(proj, gates, gdn_conv_w[0], a_log, dt_bias, gdn_norm_w)
    y = _post_gdn(x_prompt, og, mod_p[0], ln[0], w_out, w_up[0], w_down[0], tm=256)
    y_prompt, p_pool16 = _pool_layer(y, mod_p[1], ln[1], pw, pool_scale, w_up[1], w_down[1], tm=256)
    p_conv = jnp.transpose(proj[:, :N_CONVBLK, seq - (CONV_W - 1):, :], (0, 2, 1, 3)).reshape(
        nb, CONV_W - 1, CONV_DIM)
    p_pool = p_pool16[:, 1:, :]

    xs = x_sample.reshape(1, ns, D_MODEL)
    proj_s, gates_s = _proj(xs, mod_s[0], w_qkvz, w_gate, tm=ns)
    conv_state = state_gdn_conv[0].reshape(ns, (CONV_W - 1) * CONV_DIM)
    q_s, k_s, v_s, z_s, conv_new, beta_s, eg_s = _gdn_step_pre(
        proj_s, gates_s, conv_state, gdn_conv_w[0], a_log, dt_bias)
    og_s, s_S = _gdn_step(beta_s[:, :GDN_NV].reshape(-1), eg_s[:, :GDN_NV].reshape(-1), q_s, k_s, v_s, z_s,
                          gdn_norm_w, state_gdn_S[0], bt=8)
    y_s = _post_gdn(xs, og_s.reshape(1, ns, VALUE_DIM), mod_s[0], ln[0], w_out, w_up[0], w_down[0], tm=ns)
    y_sample, pool_new = _pool_step(y_s, mod_s[1], ln[1], state_pool[0].reshape(ns, POOL_BUF * D_MODEL),
                                    pw, pool_scale, w_up[1], w_down[1], pos0=PAST_LEN)

    return (y_prompt,
            y_sample.reshape(ns, 1, D_MODEL),
            p_S[None],
            p_conv[None],
            p_pool[None],
            s_S[None],
            conv_new.reshape(1, ns, CONV_W - 1, CONV_DIM),
            pool_new.reshape(1, ns, POOL_BUF, D_MODEL))
```

```python
import functools

import jax
import jax.numpy as jnp
from jax import lax
from jax.experimental import pallas as pl
from jax.experimental.pallas import tpu as pltpu

F32 = jnp.float32
BF16 = jnp.bfloat16

D_MODEL = 1024
DEPTH = 2
GDN_NK = 8
GDN_NV = 16
GDN_DK = 128
GDN_DV = 128
KEY_DIM = GDN_NK * GDN_DK
VALUE_DIM = GDN_NV * GDN_DV
CONV_W = 4
CONV_DIM = 2 * KEY_DIM + VALUE_DIM
QKVZ_DIM = CONV_DIM + VALUE_DIM
CHUNK = 64
POOL_WINDOWS = (2, 4, 8, 16)
POOL_G = D_MODEL // len(POOL_WINDOWS)
POOL_BUF = max(POOL_WINDOWS) - 1
D_FF = 2816
DN_ALPHA = (2 * DEPTH) ** 0.25
LN_EPS = 1e-5
NORM_EPS = 1e-6
PAST_LEN = 16384

LANES = 128
SUBLANES = 8
N_COLBLK = QKVZ_DIM // LANES
N_CONVBLK = CONV_DIM // LANES
GATE_W = 2 * LANES
FF_SPLIT = 2
VMEM_LIMIT = 56 * 1024 * 1024


def _bdot(a, b):
    return jnp.dot(a.astype(BF16), b.astype(BF16), preferred_element_type=F32)


def _silu(x):
    return x * jax.nn.sigmoid(x)


def _softplus(x):
    return jnp.maximum(x, 0.0) + jnp.log1p(jnp.exp(-jnp.abs(x)))


def _layer_norm(x, g, b):
    mu = jnp.mean(x, axis=-1, keepdims=True)
    xc = x - mu
    var = jnp.mean(xc * xc, axis=-1, keepdims=True)
    return xc * lax.rsqrt(var + LN_EPS) * g + b


def _const_spec(shape):
    nd = len(shape)
    return pl.BlockSpec(shape, lambda *_: (0,) * nd, pipeline_mode=pl.Buffered(1))


def _ada_kernel(c_ref, w_ref, b_ref, o_ref):
    c = c_ref[...]
    o_ref[0] = _bdot(_silu(c), w_ref[0]) + b_ref[0]


def _ada(c_all, ada_w, ada_b):
    n = c_all.shape[0]
    tn = 1536
    return pl.pallas_call(
        _ada_kernel,
        grid=(DEPTH, 6 * D_MODEL // tn),
        in_specs=[
            pl.BlockSpec((n, D_MODEL), lambda l, j: (0, 0)),
            pl.BlockSpec((1, D_MODEL, tn), lambda l, j: (l, 0, j)),
            pl.BlockSpec((1, 1, tn), lambda l, j: (l, 0, j)),
        ],
        out_specs=pl.BlockSpec((1, n, tn), lambda l, j: (l, 0, j)),
        out_shape=jax.ShapeDtypeStruct((DEPTH, n, 6 * D_MODEL), F32),
        compiler_params=pltpu.CompilerParams(
            dimension_semantics=("arbitrary", "arbitrary"), vmem_limit_bytes=VMEM_LIMIT),
        name="ada_mod",
    )(c_all, ada_w, ada_b.reshape(DEPTH, 1, 6 * D_MODEL))


def _proj_kernel(x_ref, mod_ref, w_ref, wg_ref, o_ref, g_ref):
    x = x_ref[0]
    sh = mod_ref[0, :, 0:D_MODEL]
    sc = mod_ref[0, :, D_MODEL:2 * D_MODEL]
    u = (x * (1.0 + sc) + sh).astype(BF16)
    step = 4
    for j in range(0, N_COLBLK, step):
        res = jnp.dot(u, w_ref[:, j * LANES:(j + step) * LANES], preferred_element_type=F32)
        for i in range(step):
            o_ref[0, j + i] = res[:, i * LANES:(i + 1) * LANES]
    g_ref[0] = jnp.dot(u, wg_ref[...], preferred_element_type=F32)


def _proj(x, mod, w_qkvz, w_gate, tm):
    b, l, _ = x.shape
    r = mod.shape[1]
    return pl.pallas_call(
        _proj_kernel,
        grid=(b, l // tm),
        in_specs=[
            pl.BlockSpec((1, tm, D_MODEL), lambda i, j: (i, j, 0)),
            pl.BlockSpec((1, r, 6 * D_MODEL), lambda i, j: (i, 0, 0)),
            _const_spec((D_MODEL, QKVZ_DIM)),
            _const_spec((D_MODEL, GATE_W)),
        ],
        out_specs=[
            pl.BlockSpec((1, N_COLBLK, tm, LANES), lambda i, j: (i, 0, j, 0)),
            pl.BlockSpec((1, tm, GATE_W), lambda i, j: (i, j, 0)),
        ],
        out_shape=[
            jax.ShapeDtypeStruct((b, N_COLBLK, l, LANES), F32),
            jax.ShapeDtypeStruct((b, l, GATE_W), F32),
        ],
        compiler_params=pltpu.CompilerParams(
            dimension_semantics=("arbitrary", "arbitrary"), vmem_limit_bytes=VMEM_LIMIT),
        name="gdn_proj",
    )(x, mod, w_qkvz, w_gate)


def _gdn_chunk_kernel(p_ref, g_ref, cw_ref, alog_ref, dtb_ref, nw_ref, o_ref, s_out_ref, s_scr, ext_scr):
    c_idx = pl.program_id(1)
    C = CHUNK
    HALO = SUBLANES

    @pl.when(c_idx == 0)
    def _():
        s_scr[...] = jnp.zeros_like(s_scr)
        ext_scr[:, 0:HALO, :] = jnp.zeros((N_CONVBLK, HALO, LANES), F32)

    cw = cw_ref[...]
    conv = []
    for j in range(N_CONVBLK):
        ext_scr[j, HALO:HALO + C, :] = p_ref[0, j]
        acc = None
        for t in range(CONV_W):
            term = ext_scr[j, HALO - (CONV_W - 1) + t:HALO - (CONV_W - 1) + t + C, :] * \
                cw[t:t + 1, j * LANES:(j + 1) * LANES]
            acc = term if acc is None else acc + term
        conv.append(_silu(acc))
        ext_scr[j, 0:HALO, :] = ext_scr[j, C:C + HALO, :]

    def l2n(y):
        return y * lax.rsqrt(jnp.sum(y * y, axis=-1, keepdims=True) + NORM_EPS)

    q = [l2n(conv[j]) * (GDN_DK ** -0.5) for j in range(GDN_NK)]
    k = [l2n(conv[GDN_NK + j]) for j in range(GDN_NK)]
    v = conv[2 * GDN_NK:]

    beta = jax.nn.sigmoid(g_ref[0, :, 0:LANES])
    g = -jnp.exp(alog_ref[...]) * _softplus(g_ref[0, :, LANES:2 * LANES] + dtb_ref[...])
    r2 = lax.broadcasted_iota(jnp.int32, (C, 2 * C), 0)
    c2 = lax.broadcasted_iota(jnp.int32, (C, 2 * C), 1)
    cm = jnp.where(c2 >= C, c2 - C, c2)
    left = c2 < C
    incl = r2 >= cm
    strict = r2 > cm
    ident_left = jnp.where(left & (r2 == cm), 1.0, 0.0).astype(F32)
    zeros_c = jnp.zeros((C, LANES), F32)
    ltri = jnp.where(left & incl, 1.0, 0.0).astype(F32)
    gc = jnp.dot(ltri, jnp.concatenate([g, zeros_c], axis=0), preferred_element_type=F32,
                 precision=lax.Precision.HIGHEST)
    eye = (lax.broadcasted_iota(jnp.int32, (LANES, LANES), 0) ==
           lax.broadcasted_iota(jnp.int32, (LANES, LANES), 1)).astype(F32)
    gc_rows = lax.dot_general(eye, jnp.concatenate([gc, gc], axis=0), (((1,), (1,)), ((), ())),
                              preferred_element_type=F32, precision=lax.Precision.HIGHEST)
    egc = jnp.exp(gc)
    g_last = gc[C - 1:C, :]
    edl = jnp.exp(g_last - gc)
    egl = jnp.exp(g_last)

    heads = range(GDN_NV)
    zeros_cb = jnp.zeros((C, LANES), BF16)
    kq = [jnp.concatenate([k[p], q[p]], axis=0).astype(BF16) for p in range(GDN_NK)]
    gram = [lax.dot_general(kq[p], jnp.concatenate([k[p], k[p]], axis=0).astype(BF16),
                            (((1,), (1,)), ((), ())), preferred_element_type=F32)
            for p in range(GDN_NK)]
    kqs = [jnp.dot(kq[p], jnp.concatenate([s_scr[2 * p], s_scr[2 * p + 1]], axis=1).astype(BF16),
                   preferred_element_type=F32) for p in range(GDN_NK)]
    beta_c = [beta[:, h:h + 1] for h in heads]
    egc_c = [egc[:, h:h + 1] for h in heads]
    decay = [jnp.exp(jnp.where(incl, gc[:, h:h + 1] - gc_rows[h:h + 1, :], -jnp.inf)) for h in heads]
    neg_a = [jnp.where(strict, -(beta_c[h] * gram[h // 2][0:C] * decay[h]), 0.0) for h in heads]
    attn = [jnp.where(left, gram[h // 2][C:2 * C] * decay[h], 0.0).astype(BF16) for h in heads]
    lhs = [neg_a[h].astype(BF16) for h in heads]
    rhs_b = [jnp.where(left, ident_left, neg_a[h]) for h in heads]
    for _ in range(6):
        out = [jnp.dot(lhs[h], jnp.concatenate([zeros_cb, rhs_b[h].astype(BF16)], axis=0),
                       preferred_element_type=F32) for h in heads]
        rhs_b = [jnp.where(left, rhs_b[h] + out[h], out[h]) for h in heads]
        lhs = [out[h].astype(BF16) for h in heads]
    w = [beta_c[h] * (v[h] - egc_c[h] * kqs[h // 2][0:C, (h % 2) * LANES:(h % 2 + 1) * LANES]) for h in heads]
    v_new = [jnp.dot(jnp.where(left, rhs_b[h], 0.0).astype(BF16),
                     jnp.concatenate([w[h].astype(BF16), zeros_cb], axis=0), preferred_element_type=F32)
             for h in heads]
    o_loc = [jnp.dot(attn[h], jnp.concatenate([v_new[h].astype(BF16), zeros_cb], axis=0),
                     preferred_element_type=F32) for h in heads]
    for h in heads:
        o = egc_c[h] * kqs[h // 2][C:2 * C, (h % 2) * LANES:(h % 2 + 1) * LANES] + o_loc[h]
        o = o * lax.rsqrt(jnp.mean(o * o, axis=-1, keepdims=True) + NORM_EPS) * nw_ref[...]
        o = o * _silu(p_ref[0, N_CONVBLK + h])
        o_ref[0, :, h * LANES:(h + 1) * LANES] = o.astype(BF16)
    vv = [(edl[:, h:h + 1] * v_new[h]).astype(BF16) for h in heads]
    zeros_2 = jnp.zeros((C, 2 * LANES), BF16)
    ds = [jnp.dot(jnp.concatenate([k[p], zeros_c], axis=0).T.astype(BF16),
                  jnp.concatenate([jnp.concatenate([vv[2 * p], vv[2 * p + 1]], axis=1), zeros_2], axis=0),
                  preferred_element_type=F32) for p in range(GDN_NK)]
    for h in heads:
        s_scr[h] = s_scr[h] * egl[:, h:h + 1] + ds[h // 2][:, (h % 2) * LANES:(h % 2 + 1) * LANES]

    @pl.when(c_idx == pl.num_programs(1) - 1)
    def _():
        s_out_ref[0] = s_scr[...]


def _gdn_prompt(proj, gates, conv_w, a_log, dt_bias, norm_w):
    b, _, l, _ = proj.shape
    nc = l // CHUNK
    return pl.pallas_call(
        _gdn_chunk_kernel,
        grid=(b, nc),
        in_specs=[
            pl.BlockSpec((1, N_COLBLK, CHUNK, LANES), lambda i, c: (i, 0, c, 0)),
            pl.BlockSpec((1, CHUNK, GATE_W), lambda i, c: (i, c, 0)),
            _const_spec((CONV_W, CONV_DIM)),
            _const_spec((1, LANES)),
            _const_spec((1, LANES)),
            _const_spec((1, LANES)),
        ],
        out_specs=[
            pl.BlockSpec((1, CHUNK, VALUE_DIM), lambda i, c: (i, c, 0)),
            pl.BlockSpec((1, GDN_NV, GDN_DK, GDN_DV), lambda i, c: (i, 0, 0, 0)),
        ],
        out_shape=[
            jax.ShapeDtypeStruct((b, l, VALUE_DIM), BF16),
            jax.ShapeDtypeStruct((b, GDN_NV, GDN_DK, GDN_DV), F32),
        ],
        scratch_shapes=[
            pltpu.VMEM((GDN_NV, GDN_DK, GDN_DV), F32),
            pltpu.VMEM((N_CONVBLK, CHUNK + SUBLANES, LANES), F32),
        ],
        compiler_params=pltpu.CompilerParams(
            dimension_semantics=("arbitrary", "arbitrary"), vmem_limit_bytes=VMEM_LIMIT),
        name="gdn_chunk",
    )(proj, gates, conv_w, a_log, dt_bias, norm_w)


def _mixer_norm_ffn(y_in, h_mix, mod, ln, wu_ref, wd_ref):
    ga = mod[:, 2 * D_MODEL:3 * D_MODEL]
    sh_f = mod[:, 3 * D_MODEL:4 * D_MODEL]
    sc_f = mod[:, 4 * D_MODEL:5 * D_MODEL]
    ga_f = mod[:, 5 * D_MODEL:6 * D_MODEL]
    y1 = _layer_norm(DN_ALPHA * y_in + ga * h_mix, ln[0:1], ln[1:2])
    u = (y1 * (1.0 + sc_f) + sh_f).astype(BF16)
    fw = D_FF // FF_SPLIT
    ff = None
    for s in range(FF_SPLIT):
        gt = jnp.dot(u, wu_ref[:, s * fw:(s + 1) * fw], preferred_element_type=F32)
        up = jnp.dot(u, wu_ref[:, D_FF + s * fw:D_FF + (s + 1) * fw], preferred_element_type=F32)
        part = jnp.dot((_silu(gt) * up).astype(BF16), wd_ref[s * fw:(s + 1) * fw, :],
                       preferred_element_type=F32)
        ff = part if ff is None else ff + part
    return _layer_norm(DN_ALPHA * y1 + ga_f * ff, ln[2:3], ln[3:4])


def _post_gdn_kernel(x_ref, og_ref, mod_ref, ln_ref, wo_ref, wu_ref, wd_ref, y_ref):
    h_mix = _bdot(og_ref[0], wo_ref[...])
    y_ref[0] = _mixer_norm_ffn(x_ref[0], h_mix, mod_ref[0], ln_ref[...], wu_ref, wd_ref)


def _post_gdn(x, og, mod, ln, w_out, w_up, w_down, tm):
    b, l, _ = x.shape
    r = mod.shape[1]
    return pl.pallas_call(
        _post_gdn_kernel,
        grid=(b, l // tm),
        in_specs=[
            pl.BlockSpec((1, tm, D_MODEL), lambda i, j: (i, j, 0)),
            pl.BlockSpec((1, tm, VALUE_DIM), lambda i, j: (i, j, 0)),
            pl.BlockSpec((1, r, 6 * D_MODEL), lambda i, j: (i, 0, 0)),
            _const_spec((4, D_MODEL)),
            _const_spec((VALUE_DIM, D_MODEL)),
            _const_spec((D_MODEL, 2 * D_FF)),
            _const_spec((D_FF, D_MODEL)),
        ],
        out_specs=pl.BlockSpec((1, tm, D_MODEL), lambda i, j: (i, j, 0)),
        out_shape=jax.ShapeDtypeStruct((b, l, D_MODEL), F32),
        compiler_params=pltpu.CompilerParams(
            dimension_semantics=("arbitrary", "arbitrary"), vmem_limit_bytes=VMEM_LIMIT),
        name="post_gdn_ffn",
    )(x, og, mod, ln, w_out, w_up, w_down)


def _pool_project(pooled, pw_ref, ps):
    parts = [_bdot(pooled[:, gi * POOL_G:(gi + 1) * POOL_G], pw_ref[gi]) for gi in range(len(POOL_WINDOWS))]
    return jnp.concatenate(parts, axis=1) * ps


def _pool_layer_kernel(y_ref, mod_ref, ln_ref, pw_ref, ps_ref, wu_ref, wd_ref, o_ref, pool_ref, ext_scr, *, tm):
    j = pl.program_id(1)
    HALO = 2 * SUBLANES

    @pl.when(j == 0)
    def _():
        ext_scr[0:HALO, :] = jnp.zeros((HALO, D_MODEL), F32)

    y = y_ref[0]
    mod = mod_ref[0]
    u = y * (1.0 + mod[:, D_MODEL:2 * D_MODEL]) + mod[:, 0:D_MODEL]
    ext_scr[HALO:HALO + tm, :] = u
    pos = j * tm + lax.broadcasted_iota(jnp.int32, (tm, 1), 0)
    means = []
    for gi, w in enumerate(POOL_WINDOWS):
        cols = slice(gi * POOL_G, (gi + 1) * POOL_G)
        acc = u[:, cols]
        for m in range(1, w):
            acc = acc + ext_scr[HALO - m:HALO - m + tm, cols]
        cnt = jnp.minimum(pos + 1, w).astype(F32)
        means.append(acc / cnt)
    pooled = jnp.concatenate(means, axis=1) - u
    h_mix = _pool_project(pooled, pw_ref, ps_ref[...])
    o_ref[0] = _mixer_norm_ffn(y, h_mix, mod, ln_ref[...], wu_ref, wd_ref)
    tail = ext_scr[tm:tm + HALO, :]
    ext_scr[0:HALO, :] = tail

    @pl.when(j == pl.num_programs(1) - 1)
    def _():
        pool_ref[0] = tail


def _pool_layer(y, mod, ln, pool_w, pool_scale, w_up, w_down, tm):
    b, l, _ = y.shape
    return pl.pallas_call(
        functools.partial(_pool_layer_kernel, tm=tm),
        grid=(b, l // tm),
        in_specs=[
            pl.BlockSpec((1, tm, D_MODEL), lambda i, j: (i, j, 0)),
            pl.BlockSpec((1, 1, 6 * D_MODEL), lambda i, j: (i, 0, 0)),
            _const_spec((4, D_MODEL)),
            _const_spec((len(POOL_WINDOWS), POOL_G, POOL_G)),
            _const_spec((1, D_MODEL)),
            _const_spec((D_MODEL, 2 * D_FF)),
            _const_spec((D_FF, D_MODEL)),
        ],
        out_specs=[
            pl.BlockSpec((1, tm, D_MODEL), lambda i, j: (i, j, 0)),
            pl.BlockSpec((1, 2 * SUBLANES, D_MODEL), lambda i, j: (i, 0, 0)),
        ],
        out_shape=[
            jax.ShapeDtypeStruct((b, l, D_MODEL), F32),
            jax.ShapeDtypeStruct((b, 2 * SUBLANES, D_MODEL), F32),
        ],
        scratch_shapes=[pltpu.VMEM((tm + 2 * SUBLANES, D_MODEL), F32)],
        compiler_params=pltpu.CompilerParams(
            dimension_semantics=("arbitrary", "arbitrary"), vmem_limit_bytes=VMEM_LIMIT),
        name="pool_ffn",
    )(y, mod, ln, pool_w, pool_scale, w_up, w_down)


def _gdn_step_pre_kernel(p_ref, g_ref, cs_ref, cw_ref, alog_ref, dtb_ref, q_ref, k_ref, v_ref, z_ref, cn_ref,
                         beta_ref, eg_ref):
    cw = cw_ref[...]
    for h in range(GDN_NV):
        z_ref[:, h * LANES:(h + 1) * LANES] = p_ref[0, N_CONVBLK + h]
    for j in range(N_CONVBLK):
        cols = slice(j * LANES, (j + 1) * LANES)
        cur = p_ref[0, j]
        acc = cur * cw[CONV_W - 1:CONV_W, cols]
        for t in range(CONV_W - 1):
            acc = acc + cs_ref[:, t * CONV_DIM + j * LANES:t * CONV_DIM + (j + 1) * LANES] * cw[t:t + 1, cols]
        y = _silu(acc)
        if j < 2 * GDN_NK:
            y = y * lax.rsqrt(jnp.sum(y * y, axis=-1, keepdims=True) + NORM_EPS)
        if j < GDN_NK:
            q_ref[:, cols] = y * (GDN_DK ** -0.5)
        elif j < 2 * GDN_NK:
            k_ref[:, (j - GDN_NK) * LANES:(j - GDN_NK + 1) * LANES] = y
        else:
            v_ref[:, (j - 2 * GDN_NK) * LANES:(j - 2 * GDN_NK + 1) * LANES] = y
        cn_ref[:, (CONV_W - 2) * CONV_DIM + j * LANES:(CONV_W - 2) * CONV_DIM + (j + 1) * LANES] = cur
    cn_ref[:, 0:(CONV_W - 2) * CONV_DIM] = cs_ref[:, CONV_DIM:(CONV_W - 1) * CONV_DIM]
    beta_ref[...] = jax.nn.sigmoid(g_ref[0, :, 0:LANES])
    g = -jnp.exp(alog_ref[...]) * _softplus(g_ref[0, :, LANES:2 * LANES] + dtb_ref[...])
    eg_ref[...] = jnp.exp(g)


def _gdn_step_pre(proj, gates, conv_state, conv_w, a_log, dt_bias):
    n = proj.shape[2]
    vm = pl.BlockSpec(memory_space=pltpu.VMEM)
    return pl.pallas_call(
        _gdn_step_pre_kernel,
        in_specs=[vm] * 6,
        out_specs=[vm] * 7,
        out_shape=[
            jax.ShapeDtypeStruct((n, KEY_DIM), F32),
            jax.ShapeDtypeStruct((n, KEY_DIM), F32),
            jax.ShapeDtypeStruct((n, VALUE_DIM), F32),
            jax.ShapeDtypeStruct((n, VALUE_DIM), F32),
            jax.ShapeDtypeStruct((n, (CONV_W - 1) * CONV_DIM), F32),
            jax.ShapeDtypeStruct((n, LANES), F32),
            jax.ShapeDtypeStruct((n, LANES), F32),
        ],
        compiler_params=pltpu.CompilerParams(vmem_limit_bytes=VMEM_LIMIT),
        name="gdn_step_pre",
    )(proj, gates, conv_state, conv_w, a_log, dt_bias)


def _gdn_step_kernel(beta_ref, eg_ref, q_ref, k_ref, v_ref, z_ref, nw_ref, s_ref, o_ref, sn_ref, *, bt):
    i = pl.program_id(0)
    pad = jnp.zeros((LANES - bt, LANES), F32)
    nw = nw_ref[...]
    for p in range(GDN_NK):
        cols = slice(p * LANES, (p + 1) * LANES)
        kp = k_ref[:, cols]
        qp = q_ref[:, cols]
        kt = jnp.concatenate([kp, pad], axis=0).T
        qt = jnp.concatenate([qp, pad], axis=0).T
        qk = jnp.sum(qp * kp, axis=-1, keepdims=True)
        for b in range(bt):
            kcol = kt[:, b:b + 1]
            qcol = qt[:, b:b + 1]
            for hh in range(2):
                h = 2 * p + hh
                beta = beta_ref[(i * bt + b) * GDN_NV + h]
                eg = eg_ref[(i * bt + b) * GDN_NV + h]
                hc = slice(h * LANES, (h + 1) * LANES)
                s = s_ref[b, h]
                ks = jnp.sum(s * kcol, axis=0, keepdims=True)
                qs = jnp.sum(s * qcol, axis=0, keepdims=True)
                v_new = beta * (v_ref[b:b + 1, hc] - eg * ks)
                o = eg * qs + qk[b:b + 1, :] * v_new
                sn_ref[b, h] = s * eg + kcol * v_new
                o = o * lax.rsqrt(jnp.mean(o * o, axis=-1, keepdims=True) + NORM_EPS) * nw
                o = o * _silu(z_ref[b:b + 1, hc])
                o_ref[b:b + 1, hc] = o


def _gdn_step(beta, eg, q, k, v, z, norm_w, s_state, bt):
    n = q.shape[0]
    smem = pl.BlockSpec(memory_space=pltpu.SMEM)
    return pl.pallas_call(
        functools.partial(_gdn_step_kernel, bt=bt),
        grid=(n // bt,),
        in_specs=[
            smem, smem,
            pl.BlockSpec((bt, KEY_DIM), lambda i: (i, 0)),
            pl.BlockSpec((bt, KEY_DIM), lambda i: (i, 0)),
            pl.BlockSpec((bt, VALUE_DIM), lambda i: (i, 0)),
            pl.BlockSpec((bt, VALUE_DIM), lambda i: (i, 0)),
            _const_spec((1, LANES)),
            pl.BlockSpec((bt, GDN_NV, GDN_DK, GDN_DV), lambda i: (i, 0, 0, 0)),
        ],
        out_specs=[
            pl.BlockSpec((bt, VALUE_DIM), lambda i: (i, 0)),
            pl.BlockSpec((bt, GDN_NV, GDN_DK, GDN_DV), lambda i: (i, 0, 0, 0)),
        ],
        out_shape=[
            jax.ShapeDtypeStruct((n, VALUE_DIM), F32),
            jax.ShapeDtypeStruct(s_state.shape, F32),
        ],
        compiler_params=pltpu.CompilerParams(
            dimension_semantics=("arbitrary",), vmem_limit_bytes=VMEM_LIMIT),
        name="gdn_step",
    )(beta, eg, q, k, v, z, norm_w, s_state)


def _pool_step_kernel(y_ref, mod_ref, ln_ref, ps_state_ref, pw_ref, ps_ref, wu_ref, wd_ref, o_ref, pn_ref,
                      *, pos0):
    y = y_ref[0]
    mod = mod_ref[0]
    u = y * (1.0 + mod[:, D_MODEL:2 * D_MODEL]) + mod[:, 0:D_MODEL]
    means = []
    for gi, w in enumerate(POOL_WINDOWS):
        acc = u[:, gi * POOL_G:(gi + 1) * POOL_G]
        for m in range(1, w):
            base = (POOL_BUF - m) * D_MODEL + gi * POOL_G
            acc = acc + ps_state_ref[:, base:base + POOL_G]
        means.append(acc / float(min(pos0 + 1, w)))
    pooled = jnp.concatenate(means, axis=1) - u
    h_mix = _pool_project(pooled, pw_ref, ps_ref[...])
    o_ref[0] = _mixer_norm_ffn(y, h_mix, mod, ln_ref[...], wu_ref, wd_ref)
    pn_ref[:, 0:(POOL_BUF - 1) * D_MODEL] = ps_state_ref[:, D_MODEL:POOL_BUF * D_MODEL]
    pn_ref[:, (POOL_BUF - 1) * D_MODEL:POOL_BUF * D_MODEL] = u


def _pool_step(y, mod, ln, pool_state, pool_w, pool_scale, w_up, w_down, pos0):
    n = y.shape[1]
    vm = pl.BlockSpec(memory_space=pltpu.VMEM)
    return pl.pallas_call(
        functools.partial(_pool_step_kernel, pos0=pos0),
        in_specs=[vm] * 8,
        out_specs=[vm, vm],
        out_shape=[
            jax.ShapeDtypeStruct((1, n, D_MODEL), F32),
            jax.ShapeDtypeStruct((n, POOL_BUF * D_MODEL), F32),
        ],
        compiler_params=pltpu.CompilerParams(vmem_limit_bytes=VMEM_LIMIT),
        name="pool_step_ffn",
    )(y, mod, ln, pool_state, pool_w, pool_scale, w_up, w_down)


def kernel(x_prompt, x_sample, c_prompt, c_sample, state_gdn_S, state_gdn_conv, state_pool, ada_w, ada_b,
           ln_g, ln_b, gdn_w_in, gdn_conv_w, gdn_A_log, gdn_dt_bias, gdn_norm_w, gdn_w_out, pool_w,
           pool_scale, ffn_w_up, ffn_w_down):
    nb, seq, _ = x_prompt.shape
    ns = x_sample.shape[0]

    w_in = gdn_w_in[0]
    w_qkvz = w_in[:, :QKVZ_DIM].astype(BF16)
    w_b = w_in[:, QKVZ_DIM:QKVZ_DIM + GDN_NV]
    w_a = w_in[:, QKVZ_DIM + GDN_NV:]
    lane_pad = ((0, 0), (0, LANES - GDN_NV))
    w_gate = jnp.concatenate([jnp.pad(w_b, lane_pad), jnp.pad(w_a, lane_pad)], axis=1).astype(BF16)
    a_log = jnp.pad(gdn_A_log, lane_pad)
    dt_bias = jnp.pad(gdn_dt_bias, lane_pad)
    w_out = gdn_w_out[0].astype(BF16)
    w_up = ffn_w_up.astype(BF16)
    w_down = ffn_w_down.astype(BF16)
    pw = pool_w[0].astype(BF16)
    ln = jnp.stack([ln_g[:, 0], ln_b[:, 0], ln_g[:, 1], ln_b[:, 1]], axis=1)

    mod = _ada(jnp.concatenate([c_prompt, c_sample], axis=0), ada_w, ada_b)
    mod_p = [mod[i, :nb].reshape(nb, 1, 6 * D_MODEL) for i in range(DEPTH)]
    mod_s = [mod[i, nb:].reshape(1, ns, 6 * D_MODEL) for i in range(DEPTH)]

    proj, gates = _proj(x_prompt, mod_p[0], w_qkvz, w_gate, tm=256)
    og, p_S = _gdn_prompt(proj, gates, gdn_conv_w[0], a_log, dt_bias, gdn_norm_w)
    y = _post_gdn(x_prompt, og, mod_p[0], ln[0], w_out, w_up[0], w_down[0], tm=256)
    y_prompt, p_pool16 = _pool_layer(y, mod_p[1], ln[1], pw, pool_scale, w_up[1], w_down[1], tm=256)
    p_conv = jnp.transpose(proj[:, :N_CONVBLK, seq - (CONV_W - 1):, :], (0, 2, 1, 3)).reshape(
        nb, CONV_W - 1, CONV_DIM)
    p_pool = p_pool16[:, 1:, :]

    xs = x_sample.reshape(1, ns, D_MODEL)
    proj_s, gates_s = _proj(xs, mod_s[0], w_qkvz, w_gate, tm=ns)
    conv_state = state_gdn_conv[0].reshape(ns, (CONV_W - 1) * CONV_DIM)
    q_s, k_s, v_s, z_s, conv_new, beta_s, eg_s = _gdn_step_pre(
        proj_s, gates_s, conv_state, gdn_conv_w[0], a_log, dt_bias)
    og_s, s_S = _gdn_step(beta_s[:, :GDN_NV].reshape(-1), eg_s[:, :GDN_NV].reshape(-1), q_s, k_s, v_s, z_s,
                          gdn_norm_w, state_gdn_S[0], bt=8)
    y_s = _post_gdn(xs, og_s.reshape(1, ns, VALUE_DIM), mod_s[0], ln[0], w_out, w_up[0], w_down[0], tm=ns)
    y_sample, pool_new = _pool_step(y_s, mod_s[1], ln[1], state_pool[0].reshape(ns, POOL_BUF * D_MODEL),
                                    pw, pool_scale, w_up[1], w_down[1], pos0=PAST_LEN)

    return (y_prompt,
            y_sample.reshape(ns, 1, D_MODEL),
            p_S[None],
            p_conv[None],
            p_pool[None],
            s_S[None],
            conv_new.reshape(1, ns, CONV_W - 1, CONV_DIM),
            pool_new.reshape(1, ns, POOL_BUF, D_MODEL))
```

```python
import functools

import jax
import jax.numpy as jnp
from jax import lax
from jax.experimental import pallas as pl
from jax.experimental.pallas import tpu as pltpu

F32 = jnp.float32
BF16 = jnp.bfloat16

D_MODEL = 1024
DEPTH = 2
GDN_NK = 8
GDN_NV = 16
GDN_DK = 128
GDN_DV = 128
KEY_DIM = GDN_NK * GDN_DK
VALUE_DIM = GDN_NV * GDN_DV
CONV_W = 4
CONV_DIM = 2 * KEY_DIM + VALUE_DIM
QKVZ_DIM = CONV_DIM + VALUE_DIM
CHUNK = 64
POOL_WINDOWS = (2, 4, 8, 16)
POOL_G = D_MODEL // len(POOL_WINDOWS)
POOL_BUF = max(POOL_WINDOWS) - 1
D_FF = 2816
DN_ALPHA = (2 * DEPTH) ** 0.25
LN_EPS = 1e-5
NORM_EPS = 1e-6
PAST_LEN = 16384

LANES = 128
SUBLANES = 8
N_COLBLK = QKVZ_DIM // LANES
N_CONVBLK = CONV_DIM // LANES
GATE_W = 2 * LANES
FF_SPLIT = 2
VMEM_LIMIT = 56 * 1024 * 1024


def _bdot(a, b):
    return jnp.dot(a.astype(BF16), b.astype(BF16), preferred_element_type=F32)


def _silu(x):
    return x * jax.nn.sigmoid(x)


def _softplus(x):
    return jnp.maximum(x, 0.0) + jnp.log1p(jnp.exp(-jnp.abs(x)))


def _layer_norm(x, g, b):
    mu = jnp.mean(x, axis=-1, keepdims=True)
    xc = x - mu
    var = jnp.mean(xc * xc, axis=-1, keepdims=True)
    return xc * lax.rsqrt(var + LN_EPS) * g + b


def _const_spec(shape):
    nd = len(shape)
    return pl.BlockSpec(shape, lambda *_: (0,) * nd, pipeline_mode=pl.Buffered(1))


def _ada_kernel(c_ref, w_ref, b_ref, o_ref):
    c = c_ref[...]
    o_ref[0] = _bdot(_silu(c), w_ref[0]) + b_ref[0]


def _ada(c_all, ada_w, ada_b):
    n = c_all.shape[0]
    tn = 1536
    return pl.pallas_call(
        _ada_kernel,
        grid=(DEPTH, 6 * D_MODEL // tn),
        in_specs=[
            pl.BlockSpec((n, D_MODEL), lambda l, j: (0, 0)),
            pl.BlockSpec((1, D_MODEL, tn), lambda l, j: (l, 0, j)),
            pl.BlockSpec((1, 1, tn), lambda l, j: (l, 0, j)),
        ],
        out_specs=pl.BlockSpec((1, n, tn), lambda l, j: (l, 0, j)),
        out_shape=jax.ShapeDtypeStruct((DEPTH, n, 6 * D_MODEL), F32),
        compiler_params=pltpu.CompilerParams(
            dimension_semantics=("arbitrary", "arbitrary"), vmem_limit_bytes=VMEM_LIMIT),
        name="ada_mod",
    )(c_all, ada_w, ada_b.reshape(DEPTH, 1, 6 * D_MODEL))


def _proj_kernel(x_ref, mod_ref, w_ref, wg_ref, o_ref, g_ref):
    x = x_ref[0]
    sh = mod_ref[0, :, 0:D_MODEL]
    sc = mod_ref[0, :, D_MODEL:2 * D_MODEL]
    u = (x * (1.0 + sc) + sh).astype(BF16)
    step = 4
    for j in range(0, N_COLBLK, step):
        res = jnp.dot(u, w_ref[:, j * LANES:(j + step) * LANES], preferred_element_type=F32)
        for i in range(step):
            o_ref[0, j + i] = res[:, i * LANES:(i + 1) * LANES]
    g_ref[0] = jnp.dot(u, wg_ref[...], preferred_element_type=F32)


def _proj(x, mod, w_qkvz, w_gate, tm):
    b, l, _ = x.shape
    r = mod.shape[1]
    return pl.pallas_call(
        _proj_kernel,
        grid=(b, l // tm),
        in_specs=[
            pl.BlockSpec((1, tm, D_MODEL), lambda i, j: (i, j, 0)),
            pl.BlockSpec((1, r, 6 * D_MODEL), lambda i, j: (i, 0, 0)),
            _const_spec((D_MODEL, QKVZ_DIM)),
            _const_spec((D_MODEL, GATE_W)),
        ],
        out_specs=[
            pl.BlockSpec((1, N_COLBLK, tm, LANES), lambda i, j: (i, 0, j, 0)),
            pl.BlockSpec((1, tm, GATE_W), lambda i, j: (i, j, 0)),
        ],
        out_shape=[
            jax.ShapeDtypeStruct((b, N_COLBLK, l, LANES), F32),
            jax.ShapeDtypeStruct((b, l, GATE_W), F32),
        ],
        compiler_params=pltpu.CompilerParams(
            dimension_semantics=("arbitrary", "arbitrary"), vmem_limit_bytes=VMEM_LIMIT),
        name="gdn_proj",
    )(x, mod, w_qkvz, w_gate)


def _proj_conv_kernel(x_ref, mod_ref, w_ref, wg_ref, cw_ref, qk_ref, v_ref, sz_ref, g_ref, tail_ref, ext_scr,
                      *, tm):
    j = pl.program_id(1)
    HALO = SUBLANES

    @pl.when(j == 0)
    def _():
        ext_scr[:, 0:HALO, :] = jnp.zeros((N_CONVBLK, HALO, LANES), F32)

    x = x_ref[0]
    sh = mod_ref[0, :, 0:D_MODEL]
    sc = mod_ref[0, :, D_MODEL:2 * D_MODEL]
    u = (x * (1.0 + sc) + sh).astype(BF16)
    cw = cw_ref[...]
    step = 4
    for jb in range(0, N_COLBLK, step):
        res = jnp.dot(u, w_ref[:, jb * LANES:(jb + step) * LANES], preferred_element_type=F32)
        for i in range(step):
            blk = jb + i
            r = res[:, i * LANES:(i + 1) * LANES]
            if blk >= N_CONVBLK:
                sz_ref[0, blk - N_CONVBLK] = _silu(r).astype(BF16)
                continue
            cols = slice(blk * LANES, (blk + 1) * LANES)
            ext_scr[blk, HALO:HALO + tm, :] = r
            acc = r * cw[CONV_W - 1:CONV_W, cols]
            for t in range(CONV_W - 1):
                lo = HALO - (CONV_W - 1) + t
                acc = acc + ext_scr[blk, lo:lo + tm, :] * cw[t:t + 1, cols]
            ext_scr[blk, 0:HALO, :] = ext_scr[blk, tm:tm + HALO, :]
            y = _silu(acc)
            if blk < 2 * GDN_NK:
                y = y * lax.rsqrt(jnp.sum(y * y, axis=-1, keepdims=True) + NORM_EPS)
                if blk < GDN_NK:
                    y = y * (GDN_DK ** -0.5)
                qk_ref[0, blk] = y.astype(BF16)
            else:
                v_ref[0, blk - 2 * GDN_NK] = y.astype(BF16)
    g_ref[0] = jnp.dot(u, wg_ref[...], preferred_element_type=F32)

    @pl.when(j == pl.num_programs(1) - 1)
    def _():
        tail_ref[0] = ext_scr[:, 0:HALO, :]


def _proj_conv(x, mod, w_qkvz, w_gate, conv_w, tm):
    b, l, _ = x.shape
    head_blk = lambda n: pl.BlockSpec((1, n, tm, LANES), lambda i, j: (i, 0, j, 0))
    return pl.pallas_call(
        functools.partial(_proj_conv_kernel, tm=tm),
        grid=(b, l // tm),
        in_specs=[
            pl.BlockSpec((1, tm, D_MODEL), lambda i, j: (i, j, 0)),
            pl.BlockSpec((1, 1, 6 * D_MODEL), lambda i, j: (i, 0, 0)),
            _const_spec((D_MODEL, QKVZ_DIM)),
            _const_spec((D_MODEL, GATE_W)),
            _const_spec((CONV_W, CONV_DIM)),
        ],
        out_specs=[
            head_blk(2 * GDN_NK),
            head_blk(GDN_NV),
            head_blk(GDN_NV),
            pl.BlockSpec((1, tm, GATE_W), lambda i, j: (i, j, 0)),
            pl.BlockSpec((1, N_CONVBLK, SUBLANES, LANES), lambda i, j: (i, 0, 0, 0)),
        ],
        out_shape=[
            jax.ShapeDtypeStruct((b, 2 * GDN_NK, l, LANES), BF16),
            jax.ShapeDtypeStruct((b, GDN_NV, l, LANES), BF16),
            jax.ShapeDtypeStruct((b, GDN_NV, l, LANES), BF16),
            jax.ShapeDtypeStruct((b, l, GATE_W), F32),
            jax.ShapeDtypeStruct((b, N_CONVBLK, SUBLANES, LANES), F32),
        ],
        scratch_shapes=[pltpu.VMEM((N_CONVBLK, SUBLANES + tm, LANES), F32)],
        compiler_params=pltpu.CompilerParams(
            dimension_semantics=("arbitrary", "arbitrary"), vmem_limit_bytes=VMEM_LIMIT),
        name="gdn_proj_conv",
    )(x, mod, w_qkvz, w_gate, conv_w)


def _gdn_chunk_kernel(qk_ref, v_ref, sz_ref, g_ref, alog_ref, dtb_ref, nw_ref, o_ref, s_out_ref, s_scr):
    c_idx = pl.program_id(1)
    C = CHUNK
    nseq = s_scr.shape[0]

    @pl.when(c_idx == 0)
    def _():
        s_scr[...] = jnp.zeros_like(s_scr)

    r2 = lax.broadcasted_iota(jnp.int32, (C, 2 * C), 0)
    c2 = lax.broadcasted_iota(jnp.int32, (C, 2 * C), 1)
    cm = jnp.where(c2 >= C, c2 - C, c2)
    left = c2 < C
    incl = r2 >= cm
    strict = r2 > cm
    ident_left = jnp.where(left & (r2 == cm), 1.0, 0.0).astype(F32)
    zeros_c = jnp.zeros((C, LANES), F32)
    row_c = lax.broadcasted_iota(jnp.int32, (C, LANES), 0)

    beta, gc, gc_rows, egc, edl, egl = [], [], [], [], [], []
    for s in range(nseq):
        beta.append(jax.nn.sigmoid(g_ref[s, :, 0:LANES]))
        gcs = -jnp.exp(alog_ref[...]) * _softplus(g_ref[s, :, LANES:2 * LANES] + dtb_ref[...])
        shift = 1
        while shift < C:
            gcs = gcs + jnp.where(row_c >= shift, pltpu.roll(gcs, shift, axis=0), 0.0)
            shift *= 2
        gc.append(gcs)
        gc_rows.append(jnp.concatenate([gcs, gcs], axis=0).T)
        egc.append(jnp.exp(gcs))
        g_last = gcs[C - 1:C, :]
        edl.append(jnp.exp(g_last - gcs))
        egl.append(jnp.exp(g_last))

    pairs = [(s, p) for s in range(nseq) for p in range(GDN_NK)]
    heads = [(s, h) for s in range(nseq) for h in range(GDN_NV)]
    hl = lambda h: slice((h % 2) * LANES, (h % 2 + 1) * LANES)
    zeros_cb = jnp.zeros((C, LANES), BF16)
    k = {sp: qk_ref[sp[0], GDN_NK + sp[1]] for sp in pairs}
    kq = {sp: jnp.concatenate([k[sp], qk_ref[sp[0], sp[1]]], axis=0) for sp in pairs}
    gram = {sp: lax.dot_general(kq[sp], jnp.concatenate([k[sp], k[sp]], axis=0),
                                (((1,), (1,)), ((), ())), preferred_element_type=F32)
            for sp in pairs}
    kqs = {(s, p): jnp.dot(kq[(s, p)],
                           jnp.concatenate([s_scr[s, 2 * p], s_scr[s, 2 * p + 1]], axis=1).astype(BF16),
                           preferred_element_type=F32) for (s, p) in pairs}
    beta_c = {(s, h): beta[s][:, h:h + 1] for (s, h) in heads}
    egc_c = {(s, h): egc[s][:, h:h + 1] for (s, h) in heads}
    decay = {(s, h): jnp.exp(jnp.where(incl, gc[s][:, h:h + 1] - gc_rows[s][h:h + 1, :], -jnp.inf))
             for (s, h) in heads}
    neg_a = {(s, h): jnp.where(strict, -(beta_c[(s, h)] * gram[(s, h // 2)][0:C] * decay[(s, h)]), 0.0)
             for (s, h) in heads}
    attn = {(s, h): jnp.where(left, gram[(s, h // 2)][C:2 * C] * decay[(s, h)], 0.0).astype(BF16)
            for (s, h) in heads}
    eye_top = ident_left.astype(BF16)
    um = {sh: jnp.where(left, ident_left, neg_a[sh]).astype(BF16) for sh in heads}
    for _ in range(6):
        um = {sh: jnp.dot(um[sh], jnp.concatenate([eye_top, um[sh]], axis=0),
                          preferred_element_type=F32).astype(BF16) for sh in heads}
    w = {(s, h): beta_c[(s, h)] * (v_ref[s, h].astype(F32) - egc_c[(s, h)] * kqs[(s, h // 2)][0:C, hl(h)])
         for (s, h) in heads}
    v_new = {sh: jnp.dot(um[sh], jnp.concatenate([w[sh].astype(BF16), zeros_cb], axis=0),
                         preferred_element_type=F32) for sh in heads}
    o_loc = {sh: jnp.dot(attn[sh], jnp.concatenate([v_new[sh].astype(BF16), zeros_cb], axis=0),
                         preferred_element_type=F32) for sh in heads}
    for (s, h) in heads:
        o = egc_c[(s, h)] * kqs[(s, h // 2)][C:2 * C, hl(h)] + o_loc[(s, h)]
        o = o * lax.rsqrt(jnp.mean(o * o, axis=-1, keepdims=True) + NORM_EPS) * nw_ref[...]
        o = o * sz_ref[s, h].astype(F32)
        o_ref[s, :, h * LANES:(h + 1) * LANES] = o.astype(BF16)
    vv = {(s, h): (edl[s][:, h:h + 1] * v_new[(s, h)]).astype(BF16) for (s, h) in heads}
    zeros_2 = jnp.zeros((C, 2 * LANES), BF16)
    ds = {(s, p): jnp.dot(jnp.concatenate([k[(s, p)].astype(F32), zeros_c], axis=0).T.astype(BF16),
                          jnp.concatenate([jnp.concatenate([vv[(s, 2 * p)], vv[(s, 2 * p + 1)]], axis=1),
                                           zeros_2], axis=0),
                          preferred_element_type=F32) for (s, p) in pairs}
    for (s, h) in heads:
        s_scr[s, h] = s_scr[s, h] * egl[s][:, h:h + 1] + ds[(s, h // 2)][:, hl(h)]

    @pl.when(c_idx == pl.num_programs(1) - 1)
    def _():
        s_out_ref[...] = s_scr[...]


def _gdn_prompt(qk, v, sz, gates, a_log, dt_bias, norm_w, nseq):
    b, _, l, _ = qk.shape
    nc = l // CHUNK
    head_blk = pl.BlockSpec((nseq, GDN_NV, CHUNK, LANES), lambda i, c: (i, 0, c, 0))
    return pl.pallas_call(
        _gdn_chunk_kernel,
        grid=(b // nseq, nc),
        in_specs=[
            head_blk, head_blk, head_blk,
            pl.BlockSpec((nseq, CHUNK, GATE_W), lambda i, c: (i, c, 0)),
            _const_spec((1, LANES)),
            _const_spec((1, LANES)),
            _const_spec((1, LANES)),
        ],
        out_specs=[
            pl.BlockSpec((nseq, CHUNK, VALUE_DIM), lambda i, c: (i, c, 0)),
            pl.BlockSpec((nseq, GDN_NV, GDN_DK, GDN_DV), lambda i, c: (i, 0, 0, 0)),
        ],
        out_shape=[
            jax.ShapeDtypeStruct((b, l, VALUE_DIM), BF16),
            jax.ShapeDtypeStruct((b, GDN_NV, GDN_DK, GDN_DV), F32),
        ],
        scratch_shapes=[pltpu.VMEM((nseq, GDN_NV, GDN_DK, GDN_DV), F32)],
        compiler_params=pltpu.CompilerParams(
            dimension_semantics=("arbitrary", "arbitrary"), vmem_limit_bytes=VMEM_LIMIT),
        name="gdn_chunk",
    )(qk, v, sz, gates, a_log, dt_bias, norm_w)


def _mixer_norm_ffn(y_in, h_mix, mod, ln, wu_ref, wd_ref):
    ga = mod[:, 2 * D_MODEL:3 * D_MODEL]
    sh_f = mod[:, 3 * D_MODEL:4 * D_MODEL]
    sc_f = mod[:, 4 * D_MODEL:5 * D_MODEL]
    ga_f = mod[:, 5 * D_MODEL:6 * D_MODEL]
    y1 = _layer_norm(DN_ALPHA * y_in + ga * h_mix, ln[0:1], ln[1:2])
    u = (y1 * (1.0 + sc_f) + sh_f).astype(BF16)
    fw = D_FF // FF_SPLIT
    ff = None
    for s in range(FF_SPLIT):
        gt = jnp.dot(u, wu_ref[:, s * fw:(s + 1) * fw], preferred_element_type=F32)
        up = jnp.dot(u, wu_ref[:, D_FF + s * fw:D_FF + (s + 1) * fw], preferred_element_type=F32)
        part = jnp.dot((_silu(gt) * up).astype(BF16), wd_ref[s * fw:(s + 1) * fw, :],
                       preferred_element_type=F32)
        ff = part if ff is None else ff + part
    return _layer_norm(DN_ALPHA * y1 + ga_f * ff, ln[2:3], ln[3:4])


def _post_gdn_kernel(x_ref, og_ref, mod_ref, ln_ref, wo_ref, wu_ref, wd_ref, y_ref):
    h_mix = _bdot(og_ref[0], wo_ref[...])
    y_ref[0] = _mixer_norm_ffn(x_ref[0], h_mix, mod_ref[0], ln_ref[...], wu_ref, wd_ref)


def _post_gdn(x, og, mod, ln, w_out, w_up, w_down, tm):
    b, l, _ = x.shape
    r = mod.shape[1]
    return pl.pallas_call(
        _post_gdn_kernel,
        grid=(b, l // tm),
        in_specs=[
            pl.BlockSpec((1, tm, D_MODEL), lambda i, j: (i, j, 0)),
            pl.BlockSpec((1, tm, VALUE_DIM), lambda i, j: (i, j, 0)),
            pl.BlockSpec((1, r, 6 * D_MODEL), lambda i, j: (i, 0, 0)),
            _const_spec((4, D_MODEL)),
            _const_spec((VALUE_DIM, D_MODEL)),
            _const_spec((D_MODEL, 2 * D_FF)),
            _const_spec((D_FF, D_MODEL)),
        ],
        out_specs=pl.BlockSpec((1, tm, D_MODEL), lambda i, j: (i, j, 0)),
        out_shape=jax.ShapeDtypeStruct((b, l, D_MODEL), F32),
        compiler_params=pltpu.CompilerParams(
            dimension_semantics=("arbitrary", "arbitrary"), vmem_limit_bytes=VMEM_LIMIT),
        name="post_gdn_ffn",
    )(x, og, mod, ln, w_out, w_up, w_down)


def _pool_project(pooled, pw_ref, ps):
    parts = [_bdot(pooled[:, gi * POOL_G:(gi + 1) * POOL_G], pw_ref[gi]) for gi in range(len(POOL_WINDOWS))]
    return jnp.concatenate(parts, axis=1) * ps


def _pool_layer_kernel(y_ref, mod_ref, ln_ref, pw_ref, ps_ref, wu_ref, wd_ref, o_ref, pool_ref, ext_scr, *, tm):
    j = pl.program_id(1)
    HALO = 2 * SUBLANES

    @pl.when(j == 0)
    def _():
        ext_scr[0:HALO, :] = jnp.zeros((HALO, D_MODEL), F32)

    y = y_ref[0]
    mod = mod_ref[0]
    u = y * (1.0 + mod[:, D_MODEL:2 * D_MODEL]) + mod[:, 0:D_MODEL]
    ext_scr[HALO:HALO + tm, :] = u
    pos = j * tm + lax.broadcasted_iota(jnp.int32, (tm, 1), 0)
    means = []
    for gi, w in enumerate(POOL_WINDOWS):
        cols = slice(gi * POOL_G, (gi + 1) * POOL_G)
        acc = u[:, cols]
        for m in range(1, w):
            acc = acc + ext_scr[HALO - m:HALO - m + tm, cols]
        cnt = jnp.minimum(pos + 1, w).astype(F32)
        means.append(acc / cnt)
    pooled = jnp.concatenate(means, axis=1) - u
    h_mix = _pool_project(pooled, pw_ref, ps_ref[...])
    o_ref[0] = _mixer_norm_ffn(y, h_mix, mod, ln_ref[...], wu_ref, wd_ref)
    tail = ext_scr[tm:tm + HALO, :]
    ext_scr[0:HALO, :] = tail

    @pl.when(j == pl.num_programs(1) - 1)
    def _():
        pool_ref[0] = tail


def _pool_layer(y, mod, ln, pool_w, pool_scale, w_up, w_down, tm):
    b, l, _ = y.shape
    return pl.pallas_call(
        functools.partial(_pool_layer_kernel, tm=tm),
        grid=(b, l // tm),
        in_specs=[
            pl.BlockSpec((1, tm, D_MODEL), lambda i, j: (i, j, 0)),
            pl.BlockSpec((1, 1, 6 * D_MODEL), lambda i, j: (i, 0, 0)),
            _const_spec((4, D_MODEL)),
            _const_spec((len(POOL_WINDOWS), POOL_G, POOL_G)),
            _const_spec((1, D_MODEL)),
            _const_spec((D_MODEL, 2 * D_FF)),
            _const_spec((D_FF, D_MODEL)),
        ],
        out_specs=[
            pl.BlockSpec((1, tm, D_MODEL), lambda i, j: (i, j, 0)),
            pl.BlockSpec((1, 2 * SUBLANES, D_MODEL), lambda i, j: (i, 0, 0)),
        ],
        out_shape=[
            jax.ShapeDtypeStruct((b, l, D_MODEL), F32),
            jax.ShapeDtypeStruct((b, 2 * SUBLANES, D_MODEL), F32),
        ],
        scratch_shapes=[pltpu.VMEM((tm + 2 * SUBLANES, D_MODEL), F32)],
        compiler_params=pltpu.CompilerParams(
            dimension_semantics=("arbitrary", "arbitrary"), vmem_limit_bytes=VMEM_LIMIT),
        name="pool_ffn",
    )(y, mod, ln, pool_w, pool_scale, w_up, w_down)


def _gdn_step_pre_kernel(p_ref, g_ref, cs_ref, cw_ref, alog_ref, dtb_ref, q_ref, k_ref, v_ref, z_ref, cn_ref,
                         beta_ref, eg_ref):
    cw = cw_ref[...]
    for h in range(GDN_NV):
        z_ref[:, h * LANES:(h + 1) * LANES] = p_ref[0, N_CONVBLK + h]
    for j in range(N_CONVBLK):
        cols = slice(j * LANES, (j + 1) * LANES)
        cur = p_ref[0, j]
        acc = cur * cw[CONV_W - 1:CONV_W, cols]
        for t in range(CONV_W - 1):
            acc = acc + cs_ref[:, t * CONV_DIM + j * LANES:t * CONV_DIM + (j + 1) * LANES] * cw[t:t + 1, cols]
        y = _silu(acc)
        if j < 2 * GDN_NK:
            y = y * lax.rsqrt(jnp.sum(y * y, axis=-1, keepdims=True) + NORM_EPS)
        if j < GDN_NK:
            q_ref[:, cols] = y * (GDN_DK ** -0.5)
        elif j < 2 * GDN_NK:
            k_ref[:, (j - GDN_NK) * LANES:(j - GDN_NK + 1) * LANES] = y
        else:
            v_ref[:, (j - 2 * GDN_NK) * LANES:(j - 2 * GDN_NK + 1) * LANES] = y
        cn_ref[:, (CONV_W - 2) * CONV_DIM + j * LANES:(CONV_W - 2) * CONV_DIM + (j + 1) * LANES] = cur
    cn_ref[:, 0:(CONV_W - 2) * CONV_DIM] = cs_ref[:, CONV_DIM:(CONV_W - 1) * CONV_DIM]
    beta_ref[...] = jax.nn.sigmoid(g_ref[0, :, 0:LANES])
    g = -jnp.exp(alog_ref[...]) * _softplus(g_ref[0, :, LANES:2 * LANES] + dtb_ref[...])
    eg_ref[...] = jnp.exp(g)


def _gdn_step_pre(proj, gates, conv_state, conv_w, a_log, dt_bias):
    n = proj.shape[2]
    vm = pl.BlockSpec(memory_space=pltpu.VMEM)
    return pl.pallas_call(
        _gdn_step_pre_kernel,
        in_specs=[vm] * 6,
        out_specs=[vm] * 7,
        out_shape=[
            jax.ShapeDtypeStruct((n, KEY_DIM), F32),
            jax.ShapeDtypeStruct((n, KEY_DIM), F32),
            jax.ShapeDtypeStruct((n, VALUE_DIM), F32),
            jax.ShapeDtypeStruct((n, VALUE_DIM), F32),
            jax.ShapeDtypeStruct((n, (CONV_W - 1) * CONV_DIM), F32),
            jax.ShapeDtypeStruct((n, LANES), F32),
            jax.ShapeDtypeStruct((n, LANES), F32),
        ],
        compiler_params=pltpu.CompilerParams(vmem_limit_bytes=VMEM_LIMIT),
        name="gdn_step_pre",
    )(proj, gates, conv_state, conv_w, a_log, dt_bias)


def _gdn_step_kernel(beta_ref, eg_ref, q_ref, k_ref, v_ref, z_ref, nw_ref, s_ref, o_ref, sn_ref, *, bt):
    i = pl.program_id(0)
    pad = jnp.zeros((LANES - bt, LANES), F32)
    nw = nw_ref[...]
    for p in range(GDN_NK):
        cols = slice(p * LANES, (p + 1) * LANES)
        kp = k_ref[:, cols]
        qp = q_ref[:, cols]
        kt = jnp.concatenate([kp, pad], axis=0).T
        qt = jnp.concatenate([qp, pad], axis=0).T
        qk = jnp.sum(qp * kp, axis=-1, keepdims=True)
        for b in range(bt):
            kcol = kt[:, b:b + 1]
            qcol = qt[:, b:b + 1]
            for hh in range(2):
                h = 2 * p + hh
                beta = beta_ref[(i * bt + b) * GDN_NV + h]
                eg = eg_ref[(i * bt + b) * GDN_NV + h]
                hc = slice(h * LANES, (h + 1) * LANES)
                s = s_ref[b, h]
                ks = jnp.sum(s * kcol, axis=0, keepdims=True)
                qs = jnp.sum(s * qcol, axis=0, keepdims=True)
                v_new = beta * (v_ref[b:b + 1, hc] - eg * ks)
                o = eg * qs + qk[b:b + 1, :] * v_new
                sn_ref[b, h] = s * eg + kcol * v_new
                o = o * lax.rsqrt(jnp.mean(o * o, axis=-1, keepdims=True) + NORM_EPS) * nw
                o = o * _silu(z_ref[b:b + 1, hc])
                o_ref[b:b + 1, hc] = o


def _gdn_step(beta, eg, q, k, v, z, norm_w, s_state, bt):
    n = q.shape[0]
    smem = pl.BlockSpec(memory_space=pltpu.SMEM)
    return pl.pallas_call(
        functools.partial(_gdn_step_kernel, bt=bt),
        grid=(n // bt,),
        in_specs=[
            smem, smem,
            pl.BlockSpec((bt, KEY_DIM), lambda i: (i, 0)),
            pl.BlockSpec((bt, KEY_DIM), lambda i: (i, 0)),
            pl.BlockSpec((bt, VALUE_DIM), lambda i: (i, 0)),
            pl.BlockSpec((bt, VALUE_DIM), lambda i: (i, 0)),
            _const_spec((1, LANES)),
            pl.BlockSpec((bt, GDN_NV, GDN_DK, GDN_DV), lambda i: (i, 0, 0, 0)),
        ],
        out_specs=[
            pl.BlockSpec((bt, VALUE_DIM), lambda i: (i, 0)),
            pl.BlockSpec((bt, GDN_NV, GDN_DK, GDN_DV), lambda i: (i, 0, 0, 0)),
        ],
        out_shape=[
            jax.ShapeDtypeStruct((n, VALUE_DIM), F32),
            jax.ShapeDtypeStruct(s_state.shape, F32),
        ],
        compiler_params=pltpu.CompilerParams(
            dimension_semantics=("arbitrary",), vmem_limit_bytes=VMEM_LIMIT),
        name="gdn_step",
    )(beta, eg, q, k, v, z, norm_w, s_state)


def _pool_step_kernel(y_ref, mod_ref, ln_ref, ps_state_ref, pw_ref, ps_ref, wu_ref, wd_ref, o_ref, pn_ref,
                      *, pos0):
    y = y_ref[0]
    mod = mod_ref[0]
    u = y * (1.0 + mod[:, D_MODEL:2 * D_MODEL]) + mod[:, 0:D_MODEL]
    means = []
    for gi, w in enumerate(POOL_WINDOWS):
        acc = u[:, gi * POOL_G:(gi + 1) * POOL_G]
        for m in range(1, w):
            base = (POOL_BUF - m) * D_MODEL + gi * POOL_G
            acc = acc + ps_state_ref[:, base:base + POOL_G]
        means.append(acc / float(min(pos0 + 1, w)))
    pooled = jnp.concatenate(means, axis=1) - u
    h_mix = _pool_project(pooled, pw_ref, ps_ref[...])
    o_ref[0] = _mixer_norm_ffn(y, h_mix, mod, ln_ref[...], wu_ref, wd_ref)
    pn_ref[:, 0:(POOL_BUF - 1) * D_MODEL] = ps_state_ref[:, D_MODEL:POOL_BUF * D_MODEL]
    pn_ref[:, (POOL_BUF - 1) * D_MODEL:POOL_BUF * D_MODEL] = u


def _pool_step(y, mod, ln, pool_state, pool_w, pool_scale, w_up, w_down, pos0):
    n = y.shape[1]
    vm = pl.BlockSpec(memory_space=pltpu.VMEM)
    return pl.pallas_call(
        functools.partial(_pool_step_kernel, pos0=pos0),
        in_specs=[vm] * 8,
        out_specs=[vm, vm],
        out_shape=[
            jax.ShapeDtypeStruct((1, n, D_MODEL), F32),
            jax.ShapeDtypeStruct((n, POOL_BUF * D_MODEL), F32),
        ],
        compiler_params=pltpu.CompilerParams(vmem_limit_bytes=VMEM_LIMIT),
        name="pool_step_ffn",
    )(y, mod, ln, pool_state, pool_w, pool_scale, w_up, w_down)


def kernel(x_prompt, x_sample, c_prompt, c_sample, state_gdn_S, state_gdn_conv, state_pool, ada_w, ada_b,
           ln_g, ln_b, gdn_w_in, gdn_conv_w, gdn_A_log, gdn_dt_bias, gdn_norm_w, gdn_w_out, pool_w,
           pool_scale, ffn_w_up, ffn_w_down):
    nb, seq, _ = x_prompt.shape
    ns = x_sample.shape[0]

    w_in = gdn_w_in[0]
    w_qkvz = w_in[:, :QKVZ_DIM].astype(BF16)
    w_b = w_in[:, QKVZ_DIM:QKVZ_DIM + GDN_NV]
    w_a = w_in[:, QKVZ_DIM + GDN_NV:]
    lane_pad = ((0, 0), (0, LANES - GDN_NV))
    w_gate = jnp.concatenate([jnp.pad(w_b, lane_pad), jnp.pad(w_a, lane_pad)], axis=1).astype(BF16)
    a_log = jnp.pad(gdn_A_log, lane_pad)
    dt_bias = jnp.pad(gdn_dt_bias, lane_pad)
    w_out = gdn_w_out[0].astype(BF16)
    w_up = ffn_w_up.astype(BF16)
    w_down = ffn_w_down.astype(BF16)
    pw = pool_w[0].astype(BF16)
    ln = jnp.stack([ln_g[:, 0], ln_b[:, 0], ln_g[:, 1], ln_b[:, 1]], axis=1)

    mod = _ada(jnp.concatenate([c_prompt, c_sample], axis=0), ada_w, ada_b)
    mod_p = [mod[i, :nb].reshape(nb, 1, 6 * D_MODEL) for i in range(DEPTH)]
    mod_s = [mod[i, nb:].reshape(1, ns, 6 * D_MODEL) for i in range(DEPTH)]

    qk, v_p, sz, gates, tail = _proj_conv(x_prompt, mod_p[0], w_qkvz, w_gate, gdn_conv_w[0], tm=512)
    og, p_S = _gdn_prompt(qk, v_p, sz, gates, a_log, dt_bias, gdn_norm_w, nseq=2)
    y = _post_gdn(x_prompt, og, mod_p[0], ln[0], w_out, w_up[0], w_down[0], tm=256)
    y_prompt, p_pool16 = _pool_layer(y, mod_p[1], ln[1], pw, pool_scale, w_up[1], w_down[1], tm=256)
    p_conv = jnp.transpose(tail[:, :, SUBLANES - (CONV_W - 1):, :], (0, 2, 1, 3)).reshape(
        nb, CONV_W - 1, CONV_DIM)
    p_pool = p_pool16[:, 1:, :]

    xs = x_sample.reshape(1, ns, D_MODEL)
    proj_s, gates_s = _proj(xs, mod_s[0], w_qkvz, w_gate, tm=ns)
    conv_state = state_gdn_conv[0].reshape(ns, (CONV_W - 1) * CONV_DIM)
    q_s, k_s, v_s, z_s, conv_new, beta_s, eg_s = _gdn_step_pre(
        proj_s, gates_s, conv_state, gdn_conv_w[0], a_log, dt_bias)
    og_s, s_S = _gdn_step(beta_s[:, :GDN_NV].reshape(-1), eg_s[:, :GDN_NV].reshape(-1), q_s, k_s, v_s, z_s,
                          gdn_norm_w, state_gdn_S[0], bt=8)
    y_s = _post_gdn(xs, og_s.reshape(1, ns, VALUE_DIM), mod_s[0], ln[0], w_out, w_up[0], w_down[0], tm=ns)
    y_sample, pool_new = _pool_step(y_s, mod_s[1], ln[1], state_pool[0].reshape(ns, POOL_BUF * D_MODEL),
                                    pw, pool_scale, w_up[1], w_down[1], pos0=PAST_LEN)

    return (y_prompt,
            y_sample.reshape(ns, 1, D_MODEL),
            p_S[None],
            p_conv[None],
            p_pool[None],
            s_S[None],
            conv_new.reshape(1, ns, CONV_W - 1, CONV_DIM),
            pool_new.reshape(1, ns, POOL_BUF, D_MODEL))
```

```python
import functools

import jax
import jax.numpy as jnp
from jax import lax
from jax.experimental import pallas as pl
from jax.experimental.pallas import tpu as pltpu

F32 = jnp.float32
BF16 = jnp.bfloat16

D_MODEL = 1024
DEPTH = 2
GDN_NK = 8
GDN_NV = 16
GDN_DK = 128
GDN_DV = 128
KEY_DIM = GDN_NK * GDN_DK
VALUE_DIM = GDN_NV * GDN_DV
CONV_W = 4
CONV_DIM = 2 * KEY_DIM + VALUE_DIM
QKVZ_DIM = CONV_DIM + VALUE_DIM
CHUNK = 64
POOL_WINDOWS = (2, 4, 8, 16)
POOL_G = D_MODEL // len(POOL_WINDOWS)
POOL_BUF = max(POOL_WINDOWS) - 1
D_FF = 2816
DN_ALPHA = (2 * DEPTH) ** 0.25
LN_EPS = 1e-5
NORM_EPS = 1e-6
PAST_LEN = 16384

LANES = 128
SUBLANES = 8
N_COLBLK = QKVZ_DIM // LANES
N_CONVBLK = CONV_DIM // LANES
GATE_W = 2 * LANES
FF_SPLIT = 2
VMEM_LIMIT = 56 * 1024 * 1024


def _bdot(a, b):
    return jnp.dot(a.astype(BF16), b.astype(BF16), preferred_element_type=F32)


def _silu(x):
    return x * jax.nn.sigmoid(x)


def _softplus(x):
    return jnp.maximum(x, 0.0) + jnp.log1p(jnp.exp(-jnp.abs(x)))


def _layer_norm(x, g, b):
    mu = jnp.mean(x, axis=-1, keepdims=True)
    xc = x - mu
    var = jnp.mean(xc * xc, axis=-1, keepdims=True)
    return xc * lax.rsqrt(var + LN_EPS) * g + b


def _const_spec(shape):
    nd = len(shape)
    return pl.BlockSpec(shape, lambda *_: (0,) * nd, pipeline_mode=pl.Buffered(1))


def _ada_kernel(c_ref, w_ref, b_ref, o_ref):
    c = c_ref[...]
    o_ref[0] = _bdot(_silu(c), w_ref[0]) + b_ref[0]


def _ada(c_all, ada_w, ada_b):
    n = c_all.shape[0]
    tn = 1536
    return pl.pallas_call(
        _ada_kernel,
        grid=(DEPTH, 6 * D_MODEL // tn),
        in_specs=[
            pl.BlockSpec((n, D_MODEL), lambda l, j: (0, 0)),
            pl.BlockSpec((1, D_MODEL, tn), lambda l, j: (l, 0, j)),
            pl.BlockSpec((1, 1, tn), lambda l, j: (l, 0, j)),
        ],
        out_specs=pl.BlockSpec((1, n, tn), lambda l, j: (l, 0, j)),
        out_shape=jax.ShapeDtypeStruct((DEPTH, n, 6 * D_MODEL), F32),
        compiler_params=pltpu.CompilerParams(
            dimension_semantics=("arbitrary", "arbitrary"), vmem_limit_bytes=VMEM_LIMIT),
        name="ada_mod",
    )(c_all, ada_w, ada_b.reshape(DEPTH, 1, 6 * D_MODEL))


def _proj_kernel(x_ref, mod_ref, w_ref, wg_ref, o_ref, g_ref):
    x = x_ref[0]
    sh = mod_ref[0, :, 0:D_MODEL]
    sc = mod_ref[0, :, D_MODEL:2 * D_MODEL]
    u = (x * (1.0 + sc) + sh).astype(BF16)
    step = 4
    for j in range(0, N_COLBLK, step):
        res = jnp.dot(u, w_ref[:, j * LANES:(j + step) * LANES], preferred_element_type=F32)
        for i in range(step):
            o_ref[0, j + i] = res[:, i * LANES:(i + 1) * LANES]
    g_ref[0] = jnp.dot(u, wg_ref[...], preferred_element_type=F32)


def _proj(x, mod, w_qkvz, w_gate, tm):
    b, l, _ = x.shape
    r = mod.shape[1]
    return pl.pallas_call(
        _proj_kernel,
        grid=(b, l // tm),
        in_specs=[
            pl.BlockSpec((1, tm, D_MODEL), lambda i, j: (i, j, 0)),
            pl.BlockSpec((1, r, 6 * D_MODEL), lambda i, j: (i, 0, 0)),
            _const_spec((D_MODEL, QKVZ_DIM)),
            _const_spec((D_MODEL, GATE_W)),
        ],
        out_specs=[
            pl.BlockSpec((1, N_COLBLK, tm, LANES), lambda i, j: (i, 0, j, 0)),
            pl.BlockSpec((1, tm, GATE_W), lambda i, j: (i, j, 0)),
        ],
        out_shape=[
            jax.ShapeDtypeStruct((b, N_COLBLK, l, LANES), F32),
            jax.ShapeDtypeStruct((b, l, GATE_W), F32),
        ],
        compiler_params=pltpu.CompilerParams(
            dimension_semantics=("arbitrary", "arbitrary"), vmem_limit_bytes=VMEM_LIMIT),
        name="gdn_proj",
    )(x, mod, w_qkvz, w_gate)


def _proj_conv_kernel(x_ref, mod_ref, w_ref, wg_ref, cw_ref, qk_ref, v_ref, sz_ref, g_ref, tail_ref, ext_scr,
                      *, tm):
    j = pl.program_id(1)
    HALO = SUBLANES

    @pl.when(j == 0)
    def _():
        ext_scr[:, 0:HALO, :] = jnp.zeros((N_CONVBLK, HALO, LANES), F32)

    x = x_ref[0]
    sh = mod_ref[0, :, 0:D_MODEL]
    sc = mod_ref[0, :, D_MODEL:2 * D_MODEL]
    u = (x * (1.0 + sc) + sh).astype(BF16)
    cw = cw_ref[...]
    step = 4
    conv_groups = list(range(0, N_CONVBLK, step))
    z_groups = list(range(N_CONVBLK, N_COLBLK, step))
    order = []
    for i, jb in enumerate(conv_groups):
        order.append(jb)
        if i % 2 == 1 and z_groups:
            order.append(z_groups.pop(0))
    order += z_groups
    for jb in order:
        res = jnp.dot(u, w_ref[:, jb * LANES:(jb + step) * LANES], preferred_element_type=F32)
        for i in range(step):
            blk = jb + i
            r = res[:, i * LANES:(i + 1) * LANES]
            if blk >= N_CONVBLK:
                sz_ref[0, blk - N_CONVBLK] = _silu(r).astype(BF16)
                continue
            cols = slice(blk * LANES, (blk + 1) * LANES)
            ext_scr[blk, HALO:HALO + tm, :] = r
            acc = r * cw[CONV_W - 1:CONV_W, cols]
            for t in range(CONV_W - 1):
                lo = HALO - (CONV_W - 1) + t
                acc = acc + ext_scr[blk, lo:lo + tm, :] * cw[t:t + 1, cols]
            ext_scr[blk, 0:HALO, :] = ext_scr[blk, tm:tm + HALO, :]
            y = _silu(acc)
            if blk < 2 * GDN_NK:
                y = y * lax.rsqrt(jnp.sum(y * y, axis=-1, keepdims=True) + NORM_EPS)
                if blk < GDN_NK:
                    y = y * (GDN_DK ** -0.5)
                qk_ref[0, blk] = y.astype(BF16)
            else:
                v_ref[0, blk - 2 * GDN_NK] = y.astype(BF16)
    g_ref[0] = jnp.dot(u, wg_ref[...], preferred_element_type=F32)

    @pl.when(j == pl.num_programs(1) - 1)
    def _():
        tail_ref[0] = ext_scr[:, 0:HALO, :]


def _proj_conv(x, mod, w_qkvz, w_gate, conv_w, tm):
    b, l, _ = x.shape
    head_blk = lambda n: pl.BlockSpec((1, n, tm, LANES), lambda i, j: (i, 0, j, 0))
    return pl.pallas_call(
        functools.partial(_proj_conv_kernel, tm=tm),
        grid=(b, l // tm),
        in_specs=[
            pl.BlockSpec((1, tm, D_MODEL), lambda i, j: (i, j, 0)),
            pl.BlockSpec((1, 1, 6 * D_MODEL), lambda i, j: (i, 0, 0)),
            _const_spec((D_MODEL, QKVZ_DIM)),
            _const_spec((D_MODEL, GATE_W)),
            _const_spec((CONV_W, CONV_DIM)),
        ],
        out_specs=[
            head_blk(2 * GDN_NK),
            head_blk(GDN_NV),
            head_blk(GDN_NV),
            pl.BlockSpec((1, tm, GATE_W), lambda i, j: (i, j, 0)),
            pl.BlockSpec((1, N_CONVBLK, SUBLANES, LANES), lambda i, j: (i, 0, 0, 0)),
        ],
        out_shape=[
            jax.ShapeDtypeStruct((b, 2 * GDN_NK, l, LANES), BF16),
            jax.ShapeDtypeStruct((b, GDN_NV, l, LANES), BF16),
            jax.ShapeDtypeStruct((b, GDN_NV, l, LANES), BF16),
            jax.ShapeDtypeStruct((b, l, GATE_W), F32),
            jax.ShapeDtypeStruct((b, N_CONVBLK, SUBLANES, LANES), F32),
        ],
        scratch_shapes=[pltpu.VMEM((N_CONVBLK, SUBLANES + tm, LANES), F32)],
        compiler_params=pltpu.CompilerParams(
            dimension_semantics=("arbitrary", "arbitrary"), vmem_limit_bytes=VMEM_LIMIT),
        name="gdn_proj_conv",
    )(x, mod, w_qkvz, w_gate, conv_w)


def _gdn_chunk_kernel(qk_ref, v_ref, sz_ref, g_ref, alog_ref, dtb_ref, nw_ref, o_ref, s_out_ref, s_scr):
    c_idx = pl.program_id(1)
    C = CHUNK
    nseq = s_scr.shape[0]

    @pl.when(c_idx == 0)
    def _():
        s_scr[...] = jnp.zeros_like(s_scr)

    r2 = lax.broadcasted_iota(jnp.int32, (C, 2 * C), 0)
    c2 = lax.broadcasted_iota(jnp.int32, (C, 2 * C), 1)
    cm = jnp.where(c2 >= C, c2 - C, c2)
    left = c2 < C
    incl = r2 >= cm
    strict = r2 > cm
    ident_right = jnp.where((c2 >= C) & (r2 == cm), 1.0, 0.0).astype(F32)
    zeros_c = jnp.zeros((C, LANES), F32)
    row_c = lax.broadcasted_iota(jnp.int32, (C, LANES), 0)

    beta, gc, gc_rows, egc, edl, egl = [], [], [], [], [], []
    for s in range(nseq):
        beta.append(jax.nn.sigmoid(g_ref[s, :, 0:LANES]))
        gcs = -jnp.exp(alog_ref[...]) * _softplus(g_ref[s, :, LANES:2 * LANES] + dtb_ref[...])
        shift = 1
        while shift < C:
            gcs = gcs + jnp.where(row_c >= shift, pltpu.roll(gcs, shift, axis=0), 0.0)
            shift *= 2
        gc.append(gcs)
        gc_rows.append(jnp.concatenate([gcs, gcs], axis=0).T)
        egc.append(jnp.exp(gcs))
        g_last = gcs[C - 1:C, :]
        edl.append(jnp.exp(g_last - gcs))
        egl.append(jnp.exp(g_last))

    pairs = [(s, p) for s in range(nseq) for p in range(GDN_NK)]
    heads = [(s, h) for s in range(nseq) for h in range(GDN_NV)]
    hl = lambda h: slice((h % 2) * LANES, (h % 2 + 1) * LANES)
    zeros_cb = jnp.zeros((C, LANES), BF16)
    k = {sp: qk_ref[sp[0], GDN_NK + sp[1]] for sp in pairs}
    kq = {sp: jnp.concatenate([k[sp], qk_ref[sp[0], sp[1]]], axis=0) for sp in pairs}
    gram = {sp: lax.dot_general(kq[sp], jnp.concatenate([k[sp], k[sp]], axis=0),
                                (((1,), (1,)), ((), ())), preferred_element_type=F32)
            for sp in pairs}
    kqs = {(s, h): jnp.dot(kq[(s, h // 2)], s_scr[s, h].astype(BF16), preferred_element_type=F32)
           for (s, h) in heads}
    beta_c = {(s, h): beta[s][:, h:h + 1] for (s, h) in heads}
    egc_c = {(s, h): egc[s][:, h:h + 1] for (s, h) in heads}
    decay = {(s, h): jnp.exp(jnp.where(incl, gc[s][:, h:h + 1] - gc_rows[s][h:h + 1, :], -jnp.inf))
             for (s, h) in heads}
    neg_a = {(s, h): jnp.where(strict, -(beta_c[(s, h)] * gram[(s, h // 2)][0:C] * decay[(s, h)]), 0.0)
             for (s, h) in heads}
    attn = {(s, h): (gram[(s, h // 2)][C:2 * C] * decay[(s, h)])[:, 0:C].astype(BF16)
            for (s, h) in heads}
    left_b = left.astype(BF16) > 0
    um = {sh: jnp.where(left, neg_a[sh], ident_right).astype(BF16) for sh in heads}
    for _ in range(6):
        out = {sh: jnp.dot(um[sh][:, 0:C], um[sh], preferred_element_type=F32).astype(BF16) for sh in heads}
        um = {sh: jnp.where(left_b, out[sh], out[sh] + um[sh]) for sh in heads}
    w = {sh: beta_c[sh] * (v_ref[sh[0], sh[1]].astype(F32) - egc_c[sh] * kqs[sh][0:C]) for sh in heads}
    v_new = {sh: jnp.dot(um[sh], jnp.concatenate([zeros_cb, w[sh].astype(BF16)], axis=0),
                         preferred_element_type=F32) for sh in heads}
    o_loc = {sh: jnp.dot(attn[sh], v_new[sh].astype(BF16), preferred_element_type=F32) for sh in heads}
    for (s, h) in heads:
        o = egc_c[(s, h)] * kqs[(s, h)][C:2 * C] + o_loc[(s, h)]
        o = o * lax.rsqrt(jnp.mean(o * o, axis=-1, keepdims=True) + NORM_EPS) * nw_ref[...]
        o = o * sz_ref[s, h].astype(F32)
        o_ref[s, :, h * LANES:(h + 1) * LANES] = o.astype(BF16)
    vv = {(s, h): (edl[s][:, h:h + 1] * v_new[(s, h)]).astype(BF16) for (s, h) in heads}
    ds = {(s, p): jnp.dot(jnp.concatenate([k[(s, p)].astype(F32), zeros_c], axis=0).T[:, 0:C].astype(BF16),
                          jnp.concatenate([vv[(s, 2 * p)], vv[(s, 2 * p + 1)]], axis=1),
                          preferred_element_type=F32) for (s, p) in pairs}
    for (s, h) in heads:
        s_scr[s, h] = s_scr[s, h] * egl[s][:, h:h + 1] + ds[(s, h // 2)][:, hl(h)]

    @pl.when(c_idx == pl.num_programs(1) - 1)
    def _():
        s_out_ref[...] = s_scr[...]


def _gdn_prompt(qk, v, sz, gates, a_log, dt_bias, norm_w, nseq):
    b, _, l, _ = qk.shape
    nc = l // CHUNK
    head_blk = pl.BlockSpec((nseq, GDN_NV, CHUNK, LANES), lambda i, c: (i, 0, c, 0))
    return pl.pallas_call(
        _gdn_chunk_kernel,
        grid=(b // nseq, nc),
        in_specs=[
            head_blk, head_blk, head_blk,
            pl.BlockSpec((nseq, CHUNK, GATE_W), lambda i, c: (i, c, 0)),
            _const_spec((1, LANES)),
            _const_spec((1, LANES)),
            _const_spec((1, LANES)),
        ],
        out_specs=[
            pl.BlockSpec((nseq, CHUNK, VALUE_DIM), lambda i, c: (i, c, 0)),
            pl.BlockSpec((nseq, GDN_NV, GDN_DK, GDN_DV), lambda i, c: (i, 0, 0, 0)),
        ],
        out_shape=[
            jax.ShapeDtypeStruct((b, l, VALUE_DIM), BF16),
            jax.ShapeDtypeStruct((b, GDN_NV, GDN_DK, GDN_DV), F32),
        ],
        scratch_shapes=[pltpu.VMEM((nseq, GDN_NV, GDN_DK, GDN_DV), F32)],
        compiler_params=pltpu.CompilerParams(
            dimension_semantics=("arbitrary", "arbitrary"), vmem_limit_bytes=VMEM_LIMIT),
        name="gdn_chunk",
    )(qk, v, sz, gates, a_log, dt_bias, norm_w)


def _mixer_norm_ffn(y_in, h_mix, mod, ln, wu_ref, wd_ref, nsub=1):
    rs = y_in.shape[0] // nsub
    rows = lambda a, i: a if a.shape[0] == 1 else a[i * rs:(i + 1) * rs]
    subs = range(nsub)
    ga = mod[:, 2 * D_MODEL:3 * D_MODEL]
    sh_f = mod[:, 3 * D_MODEL:4 * D_MODEL]
    sc_f = mod[:, 4 * D_MODEL:5 * D_MODEL]
    ga_f = mod[:, 5 * D_MODEL:6 * D_MODEL]
    y1 = [_layer_norm(DN_ALPHA * rows(y_in, i) + rows(ga, i) * rows(h_mix, i), ln[0:1], ln[1:2]) for i in subs]
    u = [(y1[i] * (1.0 + rows(sc_f, i)) + rows(sh_f, i)).astype(BF16) for i in subs]
    fw = D_FF // FF_SPLIT
    ff = [None] * nsub
    for s in range(FF_SPLIT):
        gt = [jnp.dot(u[i], wu_ref[:, s * fw:(s + 1) * fw], preferred_element_type=F32) for i in subs]
        up = [jnp.dot(u[i], wu_ref[:, D_FF + s * fw:D_FF + (s + 1) * fw], preferred_element_type=F32)
              for i in subs]
        part = [jnp.dot((_silu(gt[i]) * up[i]).astype(BF16), wd_ref[s * fw:(s + 1) * fw, :],
                        preferred_element_type=F32) for i in subs]
        ff = [part[i] if ff[i] is None else ff[i] + part[i] for i in subs]
    out = [_layer_norm(DN_ALPHA * y1[i] + rows(ga_f, i) * ff[i], ln[2:3], ln[3:4]) for i in subs]
    return out[0] if nsub == 1 else jnp.concatenate(out, axis=0)


def _post_gdn_kernel(x_ref, og_ref, mod_ref, ln_ref, wo_ref, wu_ref, wd_ref, y_ref, *, nsub):
    h_mix = _bdot(og_ref[0], wo_ref[...])
    y_ref[0] = _mixer_norm_ffn(x_ref[0], h_mix, mod_ref[0], ln_ref[...], wu_ref, wd_ref, nsub)


def _post_gdn(x, og, mod, ln, w_out, w_up, w_down, tm, nsub=1):
    b, l, _ = x.shape
    r = mod.shape[1]
    return pl.pallas_call(
        functools.partial(_post_gdn_kernel, nsub=nsub),
        grid=(b, l // tm),
        in_specs=[
            pl.BlockSpec((1, tm, D_MODEL), lambda i, j: (i, j, 0)),
            pl.BlockSpec((1, tm, VALUE_DIM), lambda i, j: (i, j, 0)),
            pl.BlockSpec((1, r, 6 * D_MODEL), lambda i, j: (i, 0, 0)),
            _const_spec((4, D_MODEL)),
            _const_spec((VALUE_DIM, D_MODEL)),
            _const_spec((D_MODEL, 2 * D_FF)),
            _const_spec((D_FF, D_MODEL)),
        ],
        out_specs=pl.BlockSpec((1, tm, D_MODEL), lambda i, j: (i, j, 0)),
        out_shape=jax.ShapeDtypeStruct((b, l, D_MODEL), F32),
        compiler_params=pltpu.CompilerParams(
            dimension_semantics=("arbitrary", "arbitrary"), vmem_limit_bytes=VMEM_LIMIT),
        name="post_gdn_ffn",
    )(x, og, mod, ln, w_out, w_up, w_down)


def _pool_project(pooled, pw_ref, ps):
    parts = [_bdot(pooled[:, gi * POOL_G:(gi + 1) * POOL_G], pw_ref[gi]) for gi in range(len(POOL_WINDOWS))]
    return jnp.concatenate(parts, axis=1) * ps


def _pool_layer_kernel(y_ref, mod_ref, ln_ref, pw_ref, ps_ref, wu_ref, wd_ref, o_ref, pool_ref, ext_scr,
                       *, tm, nsub):
    j = pl.program_id(1)
    HALO = 2 * SUBLANES

    @pl.when(j == 0)
    def _():
        ext_scr[0:HALO, :] = jnp.zeros((HALO, D_MODEL), F32)

    y = y_ref[0]
    mod = mod_ref[0]
    u = y * (1.0 + mod[:, D_MODEL:2 * D_MODEL]) + mod[:, 0:D_MODEL]
    ext_scr[HALO:HALO + tm, :] = u
    pos = j * tm + lax.broadcasted_iota(jnp.int32, (tm, 1), 0)
    means = []
    for gi, w in enumerate(POOL_WINDOWS):
        cols = slice(gi * POOL_G, (gi + 1) * POOL_G)
        acc = u[:, cols]
        for m in range(1, w):
            acc = acc + ext_scr[HALO - m:HALO - m + tm, cols]
        cnt = jnp.minimum(pos + 1, w).astype(F32)
        means.append(acc / cnt)
    pooled = jnp.concatenate(means, axis=1) - u
    h_mix = _pool_project(pooled, pw_ref, ps_ref[...])
    o_ref[0] = _mixer_norm_ffn(y, h_mix, mod, ln_ref[...], wu_ref, wd_ref, nsub)
    tail = ext_scr[tm:tm + HALO, :]
    ext_scr[0:HALO, :] = tail

    @pl.when(j == pl.num_programs(1) - 1)
    def _():
        pool_ref[0] = tail


def _pool_layer(y, mod, ln, pool_w, pool_scale, w_up, w_down, tm, nsub=1):
    b, l, _ = y.shape
    return pl.pallas_call(
        functools.partial(_pool_layer_kernel, tm=tm, nsub=nsub),
        grid=(b, l // tm),
        in_specs=[
            pl.BlockSpec((1, tm, D_MODEL), lambda i, j: (i, j, 0)),
            pl.BlockSpec((1, 1, 6 * D_MODEL), lambda i, j: (i, 0, 0)),
            _const_spec((4, D_MODEL)),
            _const_spec((len(POOL_WINDOWS), POOL_G, POOL_G)),
            _const_spec((1, D_MODEL)),
            _const_spec((D_MODEL, 2 * D_FF)),
            _const_spec((D_FF, D_MODEL)),
        ],
        out_specs=[
            pl.BlockSpec((1, tm, D_MODEL), lambda i, j: (i, j, 0)),
            pl.BlockSpec((1, 2 * SUBLANES, D_MODEL), lambda i, j: (i, 0, 0)),
        ],
        out_shape=[
            jax.ShapeDtypeStruct((b, l, D_MODEL), F32),
            jax.ShapeDtypeStruct((b, 2 * SUBLANES, D_MODEL), F32),
        ],
        scratch_shapes=[pltpu.VMEM((tm + 2 * SUBLANES, D_MODEL), F32)],
        compiler_params=pltpu.CompilerParams(
            dimension_semantics=("arbitrary", "arbitrary"), vmem_limit_bytes=VMEM_LIMIT),
        name="pool_ffn",
    )(y, mod, ln, pool_w, pool_scale, w_up, w_down)


def _gdn_step_pre_kernel(p_ref, g_ref, cs_ref, cw_ref, alog_ref, dtb_ref, q_ref, k_ref, v_ref, z_ref, cn_ref,
                         beta_ref, eg_ref):
    cw = cw_ref[...]
    for h in range(GDN_NV):
        z_ref[:, h * LANES:(h + 1) * LANES] = p_ref[0, N_CONVBLK + h]
    for j in range(N_CONVBLK):
        cols = slice(j * LANES, (j + 1) * LANES)
        cur = p_ref[0, j]
        acc = cur * cw[CONV_W - 1:CONV_W, cols]
        for t in range(CONV_W - 1):
            acc = acc + cs_ref[:, t * CONV_DIM + j * LANES:t * CONV_DIM + (j + 1) * LANES] * cw[t:t + 1, cols]
        y = _silu(acc)
        if j < 2 * GDN_NK:
            y = y * lax.rsqrt(jnp.sum(y * y, axis=-1, keepdims=True) + NORM_EPS)
        if j < GDN_NK:
            q_ref[:, cols] = y * (GDN_DK ** -0.5)
        elif j < 2 * GDN_NK:
            k_ref[:, (j - GDN_NK) * LANES:(j - GDN_NK + 1) * LANES] = y
        else:
            v_ref[:, (j - 2 * GDN_NK) * LANES:(j - 2 * GDN_NK + 1) * LANES] = y
        cn_ref[:, (CONV_W - 2) * CONV_DIM + j * LANES:(CONV_W - 2) * CONV_DIM + (j + 1) * LANES] = cur
    cn_ref[:, 0:(CONV_W - 2) * CONV_DIM] = cs_ref[:, CONV_DIM:(CONV_W - 1) * CONV_DIM]
    beta_ref[...] = jax.nn.sigmoid(g_ref[0, :, 0:LANES])
    g = -jnp.exp(alog_ref[...]) * _softplus(g_ref[0, :, LANES:2 * LANES] + dtb_ref[...])
    eg_ref[...] = jnp.exp(g)


def _gdn_step_pre(proj, gates, conv_state, conv_w, a_log, dt_bias):
    n = proj.shape[2]
    vm = pl.BlockSpec(memory_space=pltpu.VMEM)
    return pl.pallas_call(
        _gdn_step_pre_kernel,
        in_specs=[vm] * 6,
        out_specs=[vm] * 7,
        out_shape=[
            jax.ShapeDtypeStruct((n, KEY_DIM), F32),
            jax.ShapeDtypeStruct((n, KEY_DIM), F32),
            jax.ShapeDtypeStruct((n, VALUE_DIM), F32),
            jax.ShapeDtypeStruct((n, VALUE_DIM), F32),
            jax.ShapeDtypeStruct((n, (CONV_W - 1) * CONV_DIM), F32),
            jax.ShapeDtypeStruct((n, LANES), F32),
            jax.ShapeDtypeStruct((n, LANES), F32),
        ],
        compiler_params=pltpu.CompilerParams(vmem_limit_bytes=VMEM_LIMIT),
        name="gdn_step_pre",
    )(proj, gates, conv_state, conv_w, a_log, dt_bias)


def _gdn_step_kernel(beta_ref, eg_ref, q_ref, k_ref, v_ref, z_ref, nw_ref, s_ref, o_ref, sn_ref, *, bt):
    i = pl.program_id(0)
    pad = jnp.zeros((LANES - bt, LANES), F32)
    nw = nw_ref[...]
    for p in range(GDN_NK):
        cols = slice(p * LANES, (p + 1) * LANES)
        kp = k_ref[:, cols]
        qp = q_ref[:, cols]
        kt = jnp.concatenate([kp, pad], axis=0).T
        qt = jnp.concatenate([qp, pad], axis=0).T
        qk = jnp.sum(qp * kp, axis=-1, keepdims=True)
        for b in range(bt):
            kcol = kt[:, b:b + 1]
            qcol = qt[:, b:b + 1]
            for hh in range(2):
                h = 2 * p + hh
                beta = beta_ref[(i * bt + b) * GDN_NV + h]
                eg = eg_ref[(i * bt + b) * GDN_NV + h]
                hc = slice(h * LANES, (h + 1) * LANES)
                s = s_ref[b, h]
                ks = jnp.sum(s * kcol, axis=0, keepdims=True)
                qs = jnp.sum(s * qcol, axis=0, keepdims=True)
                v_new = beta * (v_ref[b:b + 1, hc] - eg * ks)
                o = eg * qs + qk[b:b + 1, :] * v_new
                sn_ref[b, h] = s * eg + kcol * v_new
                o = o * lax.rsqrt(jnp.mean(o * o, axis=-1, keepdims=True) + NORM_EPS) * nw
                o = o * _silu(z_ref[b:b + 1, hc])
                o_ref[b:b + 1, hc] = o


def _gdn_step(beta, eg, q, k, v, z, norm_w, s_state, bt):
    n = q.shape[0]
    smem = pl.BlockSpec(memory_space=pltpu.SMEM)
    return pl.pallas_call(
        functools.partial(_gdn_step_kernel, bt=bt),
        grid=(n // bt,),
        in_specs=[
            smem, smem,
            pl.BlockSpec((bt, KEY_DIM), lambda i: (i, 0)),
            pl.BlockSpec((bt, KEY_DIM), lambda i: (i, 0)),
            pl.BlockSpec((bt, VALUE_DIM), lambda i: (i, 0)),
            pl.BlockSpec((bt, VALUE_DIM), lambda i: (i, 0)),
            _const_spec((1, LANES)),
            pl.BlockSpec((bt, GDN_NV, GDN_DK, GDN_DV), lambda i: (i, 0, 0, 0)),
        ],
        out_specs=[
            pl.BlockSpec((bt, VALUE_DIM), lambda i: (i, 0)),
            pl.BlockSpec((bt, GDN_NV, GDN_DK, GDN_DV), lambda i: (i, 0, 0, 0)),
        ],
        out_shape=[
            jax.ShapeDtypeStruct((n, VALUE_DIM), F32),
            jax.ShapeDtypeStruct(s_state.shape, F32),
        ],
        compiler_params=pltpu.CompilerParams(
            dimension_semantics=("arbitrary",), vmem_limit_bytes=VMEM_LIMIT),
        name="gdn_step",
    )(beta, eg, q, k, v, z, norm_w, s_state)


def _pool_step_kernel(y_ref, mod_ref, ln_ref, ps_state_ref, pw_ref, ps_ref, wu_ref, wd_ref, o_ref, pn_ref,
                      *, pos0):
    y = y_ref[0]
    mod = mod_ref[0]
    u = y * (1.0 + mod[:, D_MODEL:2 * D_MODEL]) + mod[:, 0:D_MODEL]
    means = []
    for gi, w in enumerate(POOL_WINDOWS):
        acc = u[:, gi * POOL_G:(gi + 1) * POOL_G]
        for m in range(1, w):
            base = (POOL_BUF - m) * D_MODEL + gi * POOL_G
            acc = acc + ps_state_ref[:, base:base + POOL_G]
        means.append(acc / float(min(pos0 + 1, w)))
    pooled = jnp.concatenate(means, axis=1) - u
    h_mix = _pool_project(pooled, pw_ref, ps_ref[...])
    o_ref[0] = _mixer_norm_ffn(y, h_mix, mod, ln_ref[...], wu_ref, wd_ref)
    pn_ref[:, 0:(POOL_BUF - 1) * D_MODEL] = ps_state_ref[:, D_MODEL:POOL_BUF * D_MODEL]
    pn_ref[:, (POOL_BUF - 1) * D_MODEL:POOL_BUF * D_MODEL] = u


def _pool_step(y, mod, ln, pool_state, pool_w, pool_scale, w_up, w_down, pos0):
    n = y.shape[1]
    vm = pl.BlockSpec(memory_space=pltpu.VMEM)
    return pl.pallas_call(
        functools.partial(_pool_step_kernel, pos0=pos0),
        in_specs=[vm] * 8,
        out_specs=[vm, vm],
        out_shape=[
            jax.ShapeDtypeStruct((1, n, D_MODEL), F32),
            jax.ShapeDtypeStruct((n, POOL_BUF * D_MODEL), F32),
        ],
        compiler_params=pltpu.CompilerParams(vmem_limit_bytes=VMEM_LIMIT),
        name="pool_step_ffn",
    )(y, mod, ln, pool_state, pool_w, pool_scale, w_up, w_down)


def kernel(x_prompt, x_sample, c_prompt, c_sample, state_gdn_S, state_gdn_conv, state_pool, ada_w, ada_b,
           ln_g, ln_b, gdn_w_in, gdn_conv_w, gdn_A_log, gdn_dt_bias, gdn_norm_w, gdn_w_out, pool_w,
           pool_scale, ffn_w_up, ffn_w_down):
    nb, seq, _ = x_prompt.shape
    ns = x_sample.shape[0]

    w_in = gdn_w_in[0]
    w_qkvz = w_in[:, :QKVZ_DIM].astype(BF16)
    w_b = w_in[:, QKVZ_DIM:QKVZ_DIM + GDN_NV]
    w_a = w_in[:, QKVZ_DIM + GDN_NV:]
    lane_pad = ((0, 0), (0, LANES - GDN_NV))
    w_gate = jnp.concatenate([jnp.pad(w_b, lane_pad), jnp.pad(w_a, lane_pad)], axis=1).astype(BF16)
    a_log = jnp.pad(gdn_A_log, lane_pad)
    dt_bias = jnp.pad(gdn_dt_bias, lane_pad)
    w_out = gdn_w_out[0].astype(BF16)
    w_up = ffn_w_up.astype(BF16)
    w_down = ffn_w_down.astype(BF16)
    pw = pool_w[0].astype(BF16)
    ln = jnp.stack([ln_g[:, 0], ln_b[:, 0], ln_g[:, 1], ln_b[:, 1]], axis=1)

    mod = _ada(jnp.concatenate([c_prompt, c_sample], axis=0), ada_w, ada_b)
    mod_p = [mod[i, :nb].reshape(nb, 1, 6 * D_MODEL) for i in range(DEPTH)]
    mod_s = [mod[i, nb:].reshape(1, ns, 6 * D_MODEL) for i in range(DEPTH)]

    qk, v_p, sz, gates, tail = _proj_conv(x_prompt, mod_p[0], w_qkvz, w_gate, gdn_conv_w[0], tm=512)
    og, p_S = _gdn_prompt(qk, v_p, sz, gates, a_log, dt_bias, gdn_norm_w, nseq=2)
    y = _post_gdn(x_prompt, og, mod_p[0], ln[0], w_out, w_up[0], w_down[0], tm=512, nsub=2)
    y_prompt, p_pool16 = _pool_layer(y, mod_p[1], ln[1], pw, pool_scale, w_up[1], w_down[1], tm=512, nsub=2)
    p_conv = jnp.transpose(tail[:, :, SUBLANES - (CONV_W - 1):, :], (0, 2, 1, 3)).reshape(
        nb, CONV_W - 1, CONV_DIM)
    p_pool = p_pool16[:, 1:, :]

    xs = x_sample.reshape(1, ns, D_MODEL)
    proj_s, gates_s = _proj(xs, mod_s[0], w_qkvz, w_gate, tm=ns)
    conv_state = state_gdn_conv[0].reshape(ns, (CONV_W - 1) * CONV_DIM)
    q_s, k_s, v_s, z_s, conv_new, beta_s, eg_s = _gdn_step_pre(
        proj_s, gates_s, conv_state, gdn_conv_w[0], a_log, dt_bias)
    og_s, s_S = _gdn_step(beta_s[:, :GDN_NV].reshape(-1), eg_s[:, :GDN_NV].reshape(-1), q_s, k_s, v_s, z_s,
                          gdn_norm_w, state_gdn_S[0], bt=8)
    y_s = _post_gdn(xs, og_s.reshape(1, ns, VALUE_DIM), mod_s[0], ln[0], w_out, w_up[0], w_down[0], tm=ns)
    y_sample, pool_new = _pool_step(y_s, mod_s[1], ln[1], state_pool[0].reshape(ns, POOL_BUF * D_MODEL),
                                    pw, pool_scale, w_up[1], w_down[1], pos0=PAST_LEN)

    return (y_prompt,
            y_sample.reshape(ns, 1, D_MODEL),
            p_S[None],
            p_conv[None],
            p_pool[None],
            s_S[None],
            conv_new.reshape(1, ns, CONV_W - 1, CONV_DIM),
            pool_new.reshape(1, ns, POOL_BUF, D_MODEL))
```

```python
import functools

import jax
import jax.numpy as jnp
from jax import lax
from jax.experimental import pallas as pl
from jax.experimental.pallas import tpu as pltpu

F32 = jnp.float32
BF16 = jnp.bfloat16

D_MODEL = 1024
DEPTH = 2
GDN_NK = 8
GDN_NV = 16
GDN_DK = 128
GDN_DV = 128
KEY_DIM = GDN_NK * GDN_DK
VALUE_DIM = GDN_NV * GDN_DV
CONV_W = 4
CONV_DIM = 2 * KEY_DIM + VALUE_DIM
QKVZ_DIM = CONV_DIM + VALUE_DIM
CHUNK = 64
POOL_WINDOWS = (2, 4, 8, 16)
POOL_G = D_MODEL // len(POOL_WINDOWS)
POOL_BUF = max(POOL_WINDOWS) - 1
D_FF = 2816
DN_ALPHA = (2 * DEPTH) ** 0.25
LN_EPS = 1e-5
NORM_EPS = 1e-6
PAST_LEN = 16384

LANES = 128
SUBLANES = 8
N_COLBLK = QKVZ_DIM // LANES
N_CONVBLK = CONV_DIM // LANES
GATE_W = 2 * LANES
MXU_DIM = 256
FF_BOUNDS = (0, 6 * MXU_DIM, D_FF)
VMEM_LIMIT = 56 * 1024 * 1024


def _bdot(a, b):
    return jnp.dot(a.astype(BF16), b.astype(BF16), preferred_element_type=F32)


def _silu(x):
    return x * jax.nn.sigmoid(x)


def _softplus(x):
    return jnp.maximum(x, 0.0) + jnp.log1p(jnp.exp(-jnp.abs(x)))


def _layer_norm(x, g, b):
    mu = jnp.mean(x, axis=-1, keepdims=True)
    xc = x - mu
    var = jnp.mean(xc * xc, axis=-1, keepdims=True)
    return xc * lax.rsqrt(var + LN_EPS) * g + b


def _const_spec(shape):
    nd = len(shape)
    return pl.BlockSpec(shape, lambda *_: (0,) * nd, pipeline_mode=pl.Buffered(1))


def _layer_spec(shape, layer):
    nd = len(shape)
    return pl.BlockSpec((1,) + tuple(shape), lambda *_: (layer,) + (0,) * nd, pipeline_mode=pl.Buffered(1))


def _ada_kernel(c_ref, w_ref, b_ref, o_ref):
    c = c_ref[...]
    o_ref[0] = _bdot(_silu(c), w_ref[0]) + b_ref[0]


def _ada(c_all, ada_w, ada_b):
    n = c_all.shape[0]
    tn = 1536
    return pl.pallas_call(
        _ada_kernel,
        grid=(DEPTH, 6 * D_MODEL // tn),
        in_specs=[
            pl.BlockSpec((n, D_MODEL), lambda l, j: (0, 0)),
            pl.BlockSpec((1, D_MODEL, tn), lambda l, j: (l, 0, j)),
            pl.BlockSpec((1, 1, tn), lambda l, j: (l, 0, j)),
        ],
        out_specs=pl.BlockSpec((1, n, tn), lambda l, j: (l, 0, j)),
        out_shape=jax.ShapeDtypeStruct((DEPTH, n, 6 * D_MODEL), F32),
        compiler_params=pltpu.CompilerParams(
            dimension_semantics=("arbitrary", "arbitrary"), vmem_limit_bytes=VMEM_LIMIT),
        name="ada_mod",
    )(c_all, ada_w, ada_b.reshape(DEPTH, 1, 6 * D_MODEL))


def _proj_kernel(x_ref, mod_ref, w_ref, wg_ref, o_ref, g_ref):
    x = x_ref[0]
    sh = mod_ref[0, :, 0:D_MODEL]
    sc = mod_ref[0, :, D_MODEL:2 * D_MODEL]
    u = (x * (1.0 + sc) + sh).astype(BF16)
    step = 4
    for j in range(0, N_COLBLK, step):
        res = jnp.dot(u, w_ref[0, :, j * LANES:(j + step) * LANES], preferred_element_type=F32)
        for i in range(step):
            o_ref[0, j + i] = res[:, i * LANES:(i + 1) * LANES]
    g_ref[0] = jnp.dot(u, wg_ref[...], preferred_element_type=F32)


def _proj(x, mod, w_qkvz, w_gate, tm):
    b, l, _ = x.shape
    r = mod.shape[1]
    return pl.pallas_call(
        _proj_kernel,
        grid=(b, l // tm),
        in_specs=[
            pl.BlockSpec((1, tm, D_MODEL), lambda i, j: (i, j, 0)),
            pl.BlockSpec((1, r, 6 * D_MODEL), lambda i, j: (i, 0, 0)),
            _const_spec((1, D_MODEL, QKVZ_DIM)),
            _const_spec((D_MODEL, GATE_W)),
        ],
        out_specs=[
            pl.BlockSpec((1, N_COLBLK, tm, LANES), lambda i, j: (i, 0, j, 0)),
            pl.BlockSpec((1, tm, GATE_W), lambda i, j: (i, j, 0)),
        ],
        out_shape=[
            jax.ShapeDtypeStruct((b, N_COLBLK, l, LANES), F32),
            jax.ShapeDtypeStruct((b, l, GATE_W), F32),
        ],
        compiler_params=pltpu.CompilerParams(
            dimension_semantics=("arbitrary", "arbitrary"), vmem_limit_bytes=VMEM_LIMIT),
        name="gdn_proj",
    )(x, mod, w_qkvz, w_gate)


def _proj_conv_kernel(x_ref, mod_ref, w_ref, wg_ref, cw_ref, qk_ref, v_ref, sz_ref, g_ref, tail_ref, ext_scr,
                      *, tm):
    j = pl.program_id(1)
    HALO = SUBLANES

    @pl.when(j == 0)
    def _():
        ext_scr[:, 0:HALO, :] = jnp.zeros((N_CONVBLK, HALO, LANES), F32)

    x = x_ref[0]
    sh = mod_ref[0, :, 0:D_MODEL]
    sc = mod_ref[0, :, D_MODEL:2 * D_MODEL]
    u = (x * (1.0 + sc) + sh).astype(BF16)
    cw = cw_ref[...]
    step = 4
    conv_groups = list(range(0, N_CONVBLK, step))
    z_groups = list(range(N_CONVBLK, N_COLBLK, step))
    order = []
    for i, jb in enumerate(conv_groups):
        order.append(jb)
        if i % 2 == 1 and z_groups:
            order.append(z_groups.pop(0))
    order += z_groups
    def project(jb):
        return jnp.dot(u, w_ref[0, :, jb * LANES:(jb + step) * LANES], preferred_element_type=F32)

    pending = project(order[0])
    for idx, jb in enumerate(order):
        res = pending
        if idx + 1 < len(order):
            pending = project(order[idx + 1])
        for i in range(step):
            blk = jb + i
            r = res[:, i * LANES:(i + 1) * LANES]
            if blk >= N_CONVBLK:
                sz_ref[0, blk - N_CONVBLK] = _silu(r).astype(BF16)
                continue
            cols = slice(blk * LANES, (blk + 1) * LANES)
            ext_scr[blk, HALO:HALO + tm, :] = r
            acc = r * cw[CONV_W - 1:CONV_W, cols]
            for t in range(CONV_W - 1):
                lo = HALO - (CONV_W - 1) + t
                acc = acc + ext_scr[blk, lo:lo + tm, :] * cw[t:t + 1, cols]
            ext_scr[blk, 0:HALO, :] = ext_scr[blk, tm:tm + HALO, :]
            y = _silu(acc)
            if blk < 2 * GDN_NK:
                y = y * lax.rsqrt(jnp.sum(y * y, axis=-1, keepdims=True) + NORM_EPS)
                if blk < GDN_NK:
                    y = y * (GDN_DK ** -0.5)
                qk_ref[0, blk] = y.astype(BF16)
            else:
                v_ref[0, blk - 2 * GDN_NK] = y.astype(BF16)
    g_ref[0] = jnp.dot(u, wg_ref[...], preferred_element_type=F32)

    @pl.when(j == pl.num_programs(1) - 1)
    def _():
        tail_ref[0] = ext_scr[:, 0:HALO, :]


def _proj_conv(x, mod, w_qkvz, w_gate, conv_w, tm):
    b, l, _ = x.shape
    head_blk = lambda n: pl.BlockSpec((1, n, tm, LANES), lambda i, j: (i, 0, j, 0))
    return pl.pallas_call(
        functools.partial(_proj_conv_kernel, tm=tm),
        grid=(b, l // tm),
        in_specs=[
            pl.BlockSpec((1, tm, D_MODEL), lambda i, j: (i, j, 0)),
            pl.BlockSpec((1, 1, 6 * D_MODEL), lambda i, j: (i, 0, 0)),
            _const_spec((1, D_MODEL, QKVZ_DIM)),
            _const_spec((D_MODEL, GATE_W)),
            _const_spec((CONV_W, CONV_DIM)),
        ],
        out_specs=[
            head_blk(2 * GDN_NK),
            head_blk(GDN_NV),
            head_blk(GDN_NV),
            pl.BlockSpec((1, tm, GATE_W), lambda i, j: (i, j, 0)),
            pl.BlockSpec((1, N_CONVBLK, SUBLANES, LANES), lambda i, j: (i, 0, 0, 0)),
        ],
        out_shape=[
            jax.ShapeDtypeStruct((b, 2 * GDN_NK, l, LANES), BF16),
            jax.ShapeDtypeStruct((b, GDN_NV, l, LANES), BF16),
            jax.ShapeDtypeStruct((b, GDN_NV, l, LANES), BF16),
            jax.ShapeDtypeStruct((b, l, GATE_W), F32),
            jax.ShapeDtypeStruct((b, N_CONVBLK, SUBLANES, LANES), F32),
        ],
        scratch_shapes=[pltpu.VMEM((N_CONVBLK, SUBLANES + tm, LANES), F32)],
        compiler_params=pltpu.CompilerParams(
            dimension_semantics=("arbitrary", "arbitrary"), vmem_limit_bytes=VMEM_LIMIT),
        name="gdn_proj_conv",
    )(x, mod, w_qkvz, w_gate, conv_w)


def _gdn_chunk_kernel(qk_ref, v_ref, sz_ref, g_ref, alog_ref, dtb_ref, nw_ref, o_ref, s_out_ref, s_scr):
    c_idx = pl.program_id(1)
    C = CHUNK
    nseq = s_scr.shape[0]

    @pl.when(c_idx == 0)
    def _():
        s_scr[...] = jnp.zeros_like(s_scr)

    r2 = lax.broadcasted_iota(jnp.int32, (C, 2 * C), 0)
    c2 = lax.broadcasted_iota(jnp.int32, (C, 2 * C), 1)
    cm = jnp.where(c2 >= C, c2 - C, c2)
    left = c2 < C
    incl = r2 >= cm
    strict = r2 > cm
    ident_right = jnp.where((c2 >= C) & (r2 == cm), 1.0, 0.0).astype(F32)
    zeros_c = jnp.zeros((C, LANES), F32)
    row_c = lax.broadcasted_iota(jnp.int32, (C, LANES), 0)

    beta, gc, gc_rows, egc, edl, egl = [], [], [], [], [], []
    for s in range(nseq):
        beta.append(jax.nn.sigmoid(g_ref[s, :, 0:LANES]))
        gcs = -jnp.exp(alog_ref[...]) * _softplus(g_ref[s, :, LANES:2 * LANES] + dtb_ref[...])
        shift = 1
        while shift < C:
            gcs = gcs + jnp.where(row_c >= shift, pltpu.roll(gcs, shift, axis=0), 0.0)
            shift *= 2
        gc.append(gcs)
        gc_rows.append(jnp.concatenate([gcs, gcs], axis=0).T)
        egc.append(jnp.exp(gcs))
        g_last = gcs[C - 1:C, :]
        edl.append(jnp.exp(g_last - gcs))
        egl.append(jnp.exp(g_last))

    pairs = [(s, p) for s in range(nseq) for p in range(GDN_NK)]
    heads = [(s, h) for s in range(nseq) for h in range(GDN_NV)]
    hl = lambda h: slice((h % 2) * LANES, (h % 2 + 1) * LANES)
    zeros_cb = jnp.zeros((C, LANES), BF16)
    k = {sp: qk_ref[sp[0], GDN_NK + sp[1]] for sp in pairs}
    kq = {sp: jnp.concatenate([k[sp], qk_ref[sp[0], sp[1]]], axis=0) for sp in pairs}
    gram = {sp: lax.dot_general(kq[sp], jnp.concatenate([k[sp], k[sp]], axis=0),
                                (((1,), (1,)), ((), ())), preferred_element_type=F32)
            for sp in pairs}
    kqs = {(s, h): jnp.dot(kq[(s, h // 2)], s_scr[s, h].astype(BF16), preferred_element_type=F32)
           for (s, h) in heads}
    beta_c = {(s, h): beta[s][:, h:h + 1] for (s, h) in heads}
    egc_c = {(s, h): egc[s][:, h:h + 1] for (s, h) in heads}
    decay = {(s, h): jnp.exp(jnp.where(incl, gc[s][:, h:h + 1] - gc_rows[s][h:h + 1, :], -jnp.inf))
             for (s, h) in heads}
    neg_a = {(s, h): jnp.where(strict, -(beta_c[(s, h)] * gram[(s, h // 2)][0:C] * decay[(s, h)]), 0.0)
             for (s, h) in heads}
    attn = {(s, h): (gram[(s, h // 2)][C:2 * C] * decay[(s, h)])[:, 0:C].astype(BF16)
            for (s, h) in heads}
    left_b = left.astype(BF16) > 0
    um = {sh: jnp.where(left, neg_a[sh], ident_right).astype(BF16) for sh in heads}
    for _ in range(6):
        out = {sh: jnp.dot(um[sh][:, 0:C], um[sh], preferred_element_type=F32).astype(BF16) for sh in heads}
        um = {sh: jnp.where(left_b, out[sh], out[sh] + um[sh]) for sh in heads}
    w = {sh: beta_c[sh] * (v_ref[sh[0], sh[1]].astype(F32) - egc_c[sh] * kqs[sh][0:C]) for sh in heads}
    v_new = {sh: jnp.dot(um[sh], jnp.concatenate([zeros_cb, w[sh].astype(BF16)], axis=0),
                         preferred_element_type=F32) for sh in heads}
    o_loc = {sh: jnp.dot(attn[sh], v_new[sh].astype(BF16), preferred_element_type=F32) for sh in heads}
    for (s, h) in heads:
        o = egc_c[(s, h)] * kqs[(s, h)][C:2 * C] + o_loc[(s, h)]
        o = o * lax.rsqrt(jnp.mean(o * o, axis=-1, keepdims=True) + NORM_EPS) * nw_ref[...]
        o = o * sz_ref[s, h].astype(F32)
        o_ref[s, :, h * LANES:(h + 1) * LANES] = o.astype(BF16)
    vv = {(s, h): (edl[s][:, h:h + 1] * v_new[(s, h)]).astype(BF16) for (s, h) in heads}
    ds = {(s, p): jnp.dot(jnp.concatenate([k[(s, p)].astype(F32), zeros_c], axis=0).T[:, 0:C].astype(BF16),
                          jnp.concatenate([vv[(s, 2 * p)], vv[(s, 2 * p + 1)]], axis=1),
                          preferred_element_type=F32) for (s, p) in pairs}
    for (s, h) in heads:
        s_scr[s, h] = s_scr[s, h] * egl[s][:, h:h + 1] + ds[(s, h // 2)][:, hl(h)]

    @pl.when(c_idx == pl.num_programs(1) - 1)
    def _():
        s_out_ref[...] = s_scr[...]


def _gdn_prompt(qk, v, sz, gates, a_log, dt_bias, norm_w, nseq):
    b, _, l, _ = qk.shape
    nc = l // CHUNK
    head_blk = pl.BlockSpec((nseq, GDN_NV, CHUNK, LANES), lambda i, c: (i, 0, c, 0))
    return pl.pallas_call(
        _gdn_chunk_kernel,
        grid=(b // nseq, nc),
        in_specs=[
            head_blk, head_blk, head_blk,
            pl.BlockSpec((nseq, CHUNK, GATE_W), lambda i, c: (i, c, 0)),
            _const_spec((1, LANES)),
            _const_spec((1, LANES)),
            _const_spec((1, LANES)),
        ],
        out_specs=[
            pl.BlockSpec((nseq, CHUNK, VALUE_DIM), lambda i, c: (i, c, 0)),
            pl.BlockSpec((nseq, GDN_NV, GDN_DK, GDN_DV), lambda i, c: (i, 0, 0, 0)),
        ],
        out_shape=[
            jax.ShapeDtypeStruct((b, l, VALUE_DIM), BF16),
            jax.ShapeDtypeStruct((b, GDN_NV, GDN_DK, GDN_DV), F32),
        ],
        scratch_shapes=[pltpu.VMEM((nseq, GDN_NV, GDN_DK, GDN_DV), F32)],
        compiler_params=pltpu.CompilerParams(
            dimension_semantics=("arbitrary", "arbitrary"), vmem_limit_bytes=VMEM_LIMIT),
        name="gdn_chunk",
    )(qk, v, sz, gates, a_log, dt_bias, norm_w)


def _mixer_norm_ffn(y_in, h_mix, mod, ln, wu_ref, wd_ref, nsub=1):
    rs = y_in.shape[0] // nsub
    rows = lambda a, i: a if a.shape[0] == 1 else a[i * rs:(i + 1) * rs]
    subs = range(nsub)
    ga = mod[:, 2 * D_MODEL:3 * D_MODEL]
    sh_f = mod[:, 3 * D_MODEL:4 * D_MODEL]
    sc_f = mod[:, 4 * D_MODEL:5 * D_MODEL]
    ga_f = mod[:, 5 * D_MODEL:6 * D_MODEL]
    y1 = [_layer_norm(DN_ALPHA * rows(y_in, i) + rows(ga, i) * rows(h_mix, i), ln[0:1], ln[1:2]) for i in subs]
    u = [(y1[i] * (1.0 + rows(sc_f, i)) + rows(sh_f, i)).astype(BF16) for i in subs]
    ff = [None] * nsub
    for lo, hi in zip(FF_BOUNDS[:-1], FF_BOUNDS[1:]):
        gt = [jnp.dot(u[i], wu_ref[0, :, lo:hi], preferred_element_type=F32) for i in subs]
        up = [jnp.dot(u[i], wu_ref[0, :, D_FF + lo:D_FF + hi], preferred_element_type=F32) for i in subs]
        part = [jnp.dot((_silu(gt[i]) * up[i]).astype(BF16), wd_ref[0, lo:hi, :],
                        preferred_element_type=F32) for i in subs]
        ff = [part[i] if ff[i] is None else ff[i] + part[i] for i in subs]
    out = [_layer_norm(DN_ALPHA * y1[i] + rows(ga_f, i) * ff[i], ln[2:3], ln[3:4]) for i in subs]
    return out[0] if nsub == 1 else jnp.concatenate(out, axis=0)


def _post_gdn_kernel(x_ref, og_ref, mod_ref, ln_ref, wo_ref, wu_ref, wd_ref, y_ref, *, nsub):
    h_mix = _bdot(og_ref[0], wo_ref[...])
    y_ref[0] = _mixer_norm_ffn(x_ref[0], h_mix, mod_ref[0], ln_ref[...], wu_ref, wd_ref, nsub)


def _post_gdn(x, og, mod, ln, w_out, w_up, w_down, tm, layer, nsub=1):
    b, l, _ = x.shape
    r = mod.shape[1]
    return pl.pallas_call(
        functools.partial(_post_gdn_kernel, nsub=nsub),
        grid=(b, l // tm),
        in_specs=[
            pl.BlockSpec((1, tm, D_MODEL), lambda i, j: (i, j, 0)),
            pl.BlockSpec((1, tm, VALUE_DIM), lambda i, j: (i, j, 0)),
            pl.BlockSpec((1, r, 6 * D_MODEL), lambda i, j: (i, 0, 0)),
            _const_spec((4, D_MODEL)),
            _const_spec((VALUE_DIM, D_MODEL)),
            _layer_spec((D_MODEL, 2 * D_FF), layer),
            _layer_spec((D_FF, D_MODEL), layer),
        ],
        out_specs=pl.BlockSpec((1, tm, D_MODEL), lambda i, j: (i, j, 0)),
        out_shape=jax.ShapeDtypeStruct((b, l, D_MODEL), F32),
        compiler_params=pltpu.CompilerParams(
            dimension_semantics=("arbitrary", "arbitrary"), vmem_limit_bytes=VMEM_LIMIT),
        name="post_gdn_ffn",
    )(x, og, mod, ln, w_out, w_up, w_down)


def _pool_project(pooled, pw_ref, ps):
    parts = [_bdot(pooled[:, gi * POOL_G:(gi + 1) * POOL_G], pw_ref[gi]) for gi in range(len(POOL_WINDOWS))]
    return jnp.concatenate(parts, axis=1) * ps


def _pool_layer_kernel(y_ref, mod_ref, ln_ref, pw_ref, ps_ref, wu_ref, wd_ref, o_ref, pool_ref, ext_scr,
                       *, tm, nsub):
    j = pl.program_id(1)
    HALO = 2 * SUBLANES

    @pl.when(j == 0)
    def _():
        ext_scr[0:HALO, :] = jnp.zeros((HALO, D_MODEL), F32)

    y = y_ref[0]
    mod = mod_ref[0]
    u = y * (1.0 + mod[:, D_MODEL:2 * D_MODEL]) + mod[:, 0:D_MODEL]
    ext_scr[HALO:HALO + tm, :] = u
    rs = tm // nsub
    h_parts = []
    for i in range(nsub):
        r0 = i * rs
        pos = j * tm + r0 + lax.broadcasted_iota(jnp.int32, (rs, 1), 0)
        means = []
        for gi, w in enumerate(POOL_WINDOWS):
            cols = slice(gi * POOL_G, (gi + 1) * POOL_G)
            acc = u[r0:r0 + rs, cols]
            for m in range(1, w):
                acc = acc + ext_scr[HALO + r0 - m:HALO + r0 - m + rs, cols]
            cnt = jnp.minimum(pos + 1, w).astype(F32)
            means.append(acc / cnt)
        pooled = jnp.concatenate(means, axis=1) - u[r0:r0 + rs]
        h_parts.append(_pool_project(pooled, pw_ref, ps_ref[...]))
    h_mix = h_parts[0] if nsub == 1 else jnp.concatenate(h_parts, axis=0)
    o_ref[0] = _mixer_norm_ffn(y, h_mix, mod, ln_ref[...], wu_ref, wd_ref, nsub)
    tail = ext_scr[tm:tm + HALO, :]
    ext_scr[0:HALO, :] = tail

    @pl.when(j == pl.num_programs(1) - 1)
    def _():
        pool_ref[0] = tail


def _pool_layer(y, mod, ln, pool_w, pool_scale, w_up, w_down, tm, layer, nsub=1):
    b, l, _ = y.shape
    return pl.pallas_call(
        functools.partial(_pool_layer_kernel, tm=tm, nsub=nsub),
        grid=(b, l // tm),
        in_specs=[
            pl.BlockSpec((1, tm, D_MODEL), lambda i, j: (i, j, 0)),
            pl.BlockSpec((1, 1, 6 * D_MODEL), lambda i, j: (i, 0, 0)),
            _const_spec((4, D_MODEL)),
            _const_spec((len(POOL_WINDOWS), POOL_G, POOL_G)),
            _const_spec((1, D_MODEL)),
            _layer_spec((D_MODEL, 2 * D_FF), layer),
            _layer_spec((D_FF, D_MODEL), layer),
        ],
        out_specs=[
            pl.BlockSpec((1, tm, D_MODEL), lambda i, j: (i, j, 0)),
            pl.BlockSpec((1, 2 * SUBLANES, D_MODEL), lambda i, j: (i, 0, 0)),
        ],
        out_shape=[
            jax.ShapeDtypeStruct((b, l, D_MODEL), F32),
            jax.ShapeDtypeStruct((b, 2 * SUBLANES, D_MODEL), F32),
        ],
        scratch_shapes=[pltpu.VMEM((tm + 2 * SUBLANES, D_MODEL), F32)],
        compiler_params=pltpu.CompilerParams(
            dimension_semantics=("arbitrary", "arbitrary"), vmem_limit_bytes=VMEM_LIMIT),
        name="pool_ffn",
    )(y, mod, ln, pool_w, pool_scale, w_up, w_down)


def _gdn_step_pre_kernel(p_ref, g_ref, cs_ref, cw_ref, alog_ref, dtb_ref, q_ref, k_ref, v_ref, z_ref, cn_ref,
                         beta_ref, eg_ref):
    cw = cw_ref[...]
    for h in range(GDN_NV):
        z_ref[:, h * LANES:(h + 1) * LANES] = p_ref[0, N_CONVBLK + h]
    for j in range(N_CONVBLK):
        cols = slice(j * LANES, (j + 1) * LANES)
        cur = p_ref[0, j]
        acc = cur * cw[CONV_W - 1:CONV_W, cols]
        for t in range(CONV_W - 1):
            acc = acc + cs_ref[:, t, cols] * cw[t:t + 1, cols]
        y = _silu(acc)
        if j < 2 * GDN_NK:
            y = y * lax.rsqrt(jnp.sum(y * y, axis=-1, keepdims=True) + NORM_EPS)
        if j < GDN_NK:
            q_ref[:, cols] = y * (GDN_DK ** -0.5)
        elif j < 2 * GDN_NK:
            k_ref[:, (j - GDN_NK) * LANES:(j - GDN_NK + 1) * LANES] = y
        else:
            v_ref[:, (j - 2 * GDN_NK) * LANES:(j - 2 * GDN_NK + 1) * LANES] = y
        cn_ref[:, CONV_W - 2, cols] = cur
    for t in range(CONV_W - 2):
        cn_ref[:, t, :] = cs_ref[:, t + 1, :]
    beta_ref[...] = jax.nn.sigmoid(g_ref[0, :, 0:LANES])
    g = -jnp.exp(alog_ref[...]) * _softplus(g_ref[0, :, LANES:2 * LANES] + dtb_ref[...])
    eg_ref[...] = jnp.exp(g)


def _gdn_step_pre(proj, gates, conv_state, conv_w, a_log, dt_bias):
    n = proj.shape[2]
    vm = pl.BlockSpec(memory_space=pltpu.VMEM)
    return pl.pallas_call(
        _gdn_step_pre_kernel,
        in_specs=[vm] * 6,
        out_specs=[vm] * 7,
        out_shape=[
            jax.ShapeDtypeStruct((n, KEY_DIM), F32),
            jax.ShapeDtypeStruct((n, KEY_DIM), F32),
            jax.ShapeDtypeStruct((n, VALUE_DIM), F32),
            jax.ShapeDtypeStruct((n, VALUE_DIM), F32),
            jax.ShapeDtypeStruct((n, CONV_W - 1, CONV_DIM), F32),
            jax.ShapeDtypeStruct((n, LANES), F32),
            jax.ShapeDtypeStruct((n, LANES), F32),
        ],
        compiler_params=pltpu.CompilerParams(vmem_limit_bytes=VMEM_LIMIT),
        name="gdn_step_pre",
    )(proj, gates, conv_state, conv_w, a_log, dt_bias)


def _gdn_step_kernel(beta_ref, eg_ref, q_ref, k_ref, v_ref, z_ref, nw_ref, s_ref, o_ref, sn_ref, *, bt):
    i = pl.program_id(0)
    pad = jnp.zeros((LANES - bt, LANES), F32)
    nw = nw_ref[...]
    for p in range(GDN_NK):
        cols = slice(p * LANES, (p + 1) * LANES)
        kp = k_ref[:, cols]
        qp = q_ref[:, cols]
        kt = jnp.concatenate([kp, pad], axis=0).T
        qt = jnp.concatenate([qp, pad], axis=0).T
        qk = jnp.sum(qp * kp, axis=-1, keepdims=True)
        for b in range(bt):
            kcol = kt[:, b:b + 1]
            qcol = qt[:, b:b + 1]
            for hh in range(2):
                h = 2 * p + hh
                beta = beta_ref[(i * bt + b) * GDN_NV + h]
                eg = eg_ref[(i * bt + b) * GDN_NV + h]
                hc = slice(h * LANES, (h + 1) * LANES)
                s = s_ref[b, h]
                ks = jnp.sum(s * kcol, axis=0, keepdims=True)
                qs = jnp.sum(s * qcol, axis=0, keepdims=True)
                v_new = beta * (v_ref[b:b + 1, hc] - eg * ks)
                o = eg * qs + qk[b:b + 1, :] * v_new
                sn_ref[b, h] = s * eg + kcol * v_new
                o = o * lax.rsqrt(jnp.mean(o * o, axis=-1, keepdims=True) + NORM_EPS) * nw
                o = o * _silu(z_ref[b:b + 1, hc])
                o_ref[b:b + 1, hc] = o


def _gdn_step(beta, eg, q, k, v, z, norm_w, s_state, bt):
    n = q.shape[0]
    smem = pl.BlockSpec(memory_space=pltpu.SMEM)
    return pl.pallas_call(
        functools.partial(_gdn_step_kernel, bt=bt),
        grid=(n // bt,),
        in_specs=[
            smem, smem,
            pl.BlockSpec((bt, KEY_DIM), lambda i: (i, 0)),
            pl.BlockSpec((bt, KEY_DIM), lambda i: (i, 0)),
            pl.BlockSpec((bt, VALUE_DIM), lambda i: (i, 0)),
            pl.BlockSpec((bt, VALUE_DIM), lambda i: (i, 0)),
            _const_spec((1, LANES)),
            pl.BlockSpec((bt, GDN_NV, GDN_DK, GDN_DV), lambda i: (i, 0, 0, 0)),
        ],
        out_specs=[
            pl.BlockSpec((bt, VALUE_DIM), lambda i: (i, 0)),
            pl.BlockSpec((bt, GDN_NV, GDN_DK, GDN_DV), lambda i: (i, 0, 0, 0)),
        ],
        out_shape=[
            jax.ShapeDtypeStruct((n, VALUE_DIM), F32),
            jax.ShapeDtypeStruct(s_state.shape, F32),
        ],
        compiler_params=pltpu.CompilerParams(
            dimension_semantics=("arbitrary",), vmem_limit_bytes=VMEM_LIMIT),
        name="gdn_step",
    )(beta, eg, q, k, v, z, norm_w, s_state)


def _pool_step_kernel(y_ref, mod_ref, ln_ref, ps_state_ref, pw_ref, ps_ref, wu_ref, wd_ref, o_ref, pn_ref,
                      *, pos0):
    y = y_ref[0]
    mod = mod_ref[0]
    u = y * (1.0 + mod[:, D_MODEL:2 * D_MODEL]) + mod[:, 0:D_MODEL]
    means = []
    for gi, w in enumerate(POOL_WINDOWS):
        cols = slice(gi * POOL_G, (gi + 1) * POOL_G)
        acc = u[:, cols]
        for m in range(1, w):
            acc = acc + ps_state_ref[:, POOL_BUF - m, cols]
        means.append(acc / float(min(pos0 + 1, w)))
    pooled = jnp.concatenate(means, axis=1) - u
    h_mix = _pool_project(pooled, pw_ref, ps_ref[...])
    o_ref[0] = _mixer_norm_ffn(y, h_mix, mod, ln_ref[...], wu_ref, wd_ref)
    for m in range(POOL_BUF - 1):
        pn_ref[:, m, :] = ps_state_ref[:, m + 1, :]
    pn_ref[:, POOL_BUF - 1, :] = u


def _pool_step(y, mod, ln, pool_state, pool_w, pool_scale, w_up, w_down, pos0, layer):
    n = y.shape[1]
    full = lambda a: _const_spec(a.shape)
    return pl.pallas_call(
        functools.partial(_pool_step_kernel, pos0=pos0),
        grid=(1,),
        in_specs=[full(y), full(mod), full(ln), full(pool_state), full(pool_w), full(pool_scale),
                  _layer_spec((D_MODEL, 2 * D_FF), layer), _layer_spec((D_FF, D_MODEL), layer)],
        out_specs=[pl.BlockSpec((1, n, D_MODEL), lambda i: (0, 0, 0)),
                   pl.BlockSpec((n, POOL_BUF, D_MODEL), lambda i: (0, 0, 0))],
        out_shape=[
            jax.ShapeDtypeStruct((1, n, D_MODEL), F32),
            jax.ShapeDtypeStruct((n, POOL_BUF, D_MODEL), F32),
        ],
        compiler_params=pltpu.CompilerParams(
            dimension_semantics=("arbitrary",), vmem_limit_bytes=VMEM_LIMIT),
        name="pool_step_ffn",
    )(y, mod, ln, pool_state, pool_w, pool_scale, w_up, w_down)


def kernel(x_prompt, x_sample, c_prompt, c_sample, state_gdn_S, state_gdn_conv, state_pool, ada_w, ada_b,
           ln_g, ln_b, gdn_w_in, gdn_conv_w, gdn_A_log, gdn_dt_bias, gdn_norm_w, gdn_w_out, pool_w,
           pool_scale, ffn_w_up, ffn_w_down):
    nb, seq, _ = x_prompt.shape
    ns = x_sample.shape[0]

    w_qkvz = gdn_w_in.astype(BF16)
    w_b = gdn_w_in[0, :, QKVZ_DIM:QKVZ_DIM + GDN_NV]
    w_a = gdn_w_in[0, :, QKVZ_DIM + GDN_NV:]
    lane_pad = ((0, 0), (0, LANES - GDN_NV))
    w_gate = jnp.concatenate([jnp.pad(w_b, lane_pad), jnp.pad(w_a, lane_pad)], axis=1).astype(BF16)
    a_log = jnp.pad(gdn_A_log, lane_pad)
    dt_bias = jnp.pad(gdn_dt_bias, lane_pad)
    w_out = gdn_w_out[0].astype(BF16)
    w_up = ffn_w_up.astype(BF16)
    w_down = ffn_w_down.astype(BF16)
    pw = pool_w[0].astype(BF16)
    ln = jnp.stack([ln_g[:, 0], ln_b[:, 0], ln_g[:, 1], ln_b[:, 1]], axis=1)

    mod = _ada(jnp.concatenate([c_prompt, c_sample], axis=0), ada_w, ada_b)
    mod_p = [mod[i, :nb].reshape(nb, 1, 6 * D_MODEL) for i in range(DEPTH)]
    mod_s = [mod[i, nb:].reshape(1, ns, 6 * D_MODEL) for i in range(DEPTH)]

    qk, v_p, sz, gates, tail = _proj_conv(x_prompt, mod_p[0], w_qkvz, w_gate, gdn_conv_w[0], tm=512)
    og, p_S = _gdn_prompt(qk, v_p, sz, gates, a_log, dt_bias, gdn_norm_w, nseq=2)
    y = _post_gdn(x_prompt, og, mod_p[0], ln[0], w_out, w_up, w_down, tm=512, layer=0, nsub=2)
    y_prompt, p_pool16 = _pool_layer(y, mod_p[1], ln[1], pw, pool_scale, w_up, w_down, tm=512, layer=1, nsub=2)
    p_conv = jnp.transpose(tail[:, :, SUBLANES - (CONV_W - 1):, :], (0, 2, 1, 3)).reshape(
        nb, CONV_W - 1, CONV_DIM)
    p_pool = p_pool16[:, 1:, :]

    xs = x_sample.reshape(1, ns, D_MODEL)
    proj_s, gates_s = _proj(xs, mod_s[0], w_qkvz, w_gate, tm=ns)
    q_s, k_s, v_s, z_s, conv_new, beta_s, eg_s = _gdn_step_pre(
        proj_s, gates_s, state_gdn_conv[0], gdn_conv_w[0], a_log, dt_bias)
    og_s, s_S = _gdn_step(beta_s[:, :GDN_NV].reshape(-1), eg_s[:, :GDN_NV].reshape(-1), q_s, k_s, v_s, z_s,
                          gdn_norm_w, state_gdn_S[0], bt=8)
    y_s = _post_gdn(xs, og_s.reshape(1, ns, VALUE_DIM), mod_s[0], ln[0], w_out, w_up, w_down, tm=ns, layer=0)
    y_sample, pool_new = _pool_step(y_s, mod_s[1], ln[1], state_pool[0], pw, pool_scale, w_up, w_down,
                                    pos0=PAST_LEN, layer=1)

    return (y_prompt,
            y_sample.reshape(ns, 1, D_MODEL),
            p_S[None],
            p_conv[None],
            p_pool[None],
            s_S[None],
            conv_new[None],
            pool_new[None])
```
